```python
import jax
import jax.numpy as jnp
from jax import lax
import numpy as np

D_MODEL = 1024
BATCH = 4
SEQ = 4096
DEPTH = 4

GRID_W = 64
CTX_LEN = 256
M_HEADS = 4
M_HEAD_DIM = 256
M_WIDTH = M_HEADS * M_HEAD_DIM
M_CHUNK = 64
M_CONV = 3
A_HEADS = 8
A_NOPE = 128
A_ROPE = 64
A_VDIM = 128
A_QRANK = 384
A_KVRANK = 256
A_QBLOCK = 128
AXIS_ROT = A_ROPE // 2
ROPE_THETA = 10000.0
D_FF = 2816
N_BRANCH = 2
N_MOD = 9
EPS = 1e-6

IN_GROUPS = (('m_q', M_WIDTH), ('m_k', M_WIDTH), ('m_v', M_WIDTH), ('m_o', M_WIDTH), ('m_gate', 4 * M_HEADS), ('a_cq', A_QRANK), ('a_ckv', A_KVRANK), ('a_kr', A_ROPE), ('br_gate', N_BRANCH * D_MODEL))
IN_WIDTH = 4 * M_WIDTH + 4 * M_HEADS + A_QRANK + A_KVRANK + A_ROPE + N_BRANCH * D_MODEL
ALL_GROUPS = ('m_q', 'm_k', 'm_v', 'm_o', 'm_gate', 'a_cq', 'a_ckv', 'a_kr', 'br_gate')
CTX_KV_GROUPS = ('m_k', 'm_v', 'm_gate', 'a_ckv', 'a_kr')

kernel_name = 'hybrid_mlstm_mla_macaron_dit'


def _rmsnorm(x, g):
    xf = x.astype(jnp.float32)
    y = xf * lax.rsqrt(jnp.mean(xf * xf, axis=-1, keepdims=True) + EPS)
    return (y * g.astype(jnp.float32)).astype(x.dtype)


def _modulate(h, shift, scale):
    return h * (1 + scale) + shift


def _swiglu(h, w_up, w_dn):
    a, b = jnp.split(h @ w_up, 2, axis=-1)
    return (jax.nn.silu(a) * b) @ w_dn


def _short_conv(x, w):
    pad = M_CONV // 2
    T = x.shape[1]
    xp = jnp.pad(x, ((0, 0), (pad, pad), (0, 0)))
    return sum(xp[:, i:i + T] * w[i] for i in range(M_CONV))


def _project(h, w_in, names):
    sizes = dict(IN_GROUPS)
    offs, o = {}, 0
    for name, n in IN_GROUPS:
        offs[name] = o
        o += n
    w = w_in if names == ALL_GROUPS else jnp.concatenate([w_in[:, offs[k]:offs[k] + sizes[k]] for k in names], axis=1)
    z = h @ w
    cuts = np.cumsum([sizes[k] for k in names])[:-1].tolist()
    return dict(zip(names, jnp.split(z, cuts, axis=-1)))


def _rope_tables(row, col):
    inv = ROPE_THETA ** (-jnp.arange(0, AXIS_ROT, 2, dtype=jnp.float32) / AXIS_ROT)
    ang = jnp.concatenate([row[:, None] * inv, col[:, None] * inv], axis=-1)
    return jnp.cos(ang), jnp.sin(ang)


def _rope(x, cos, sin):
    xr = x.astype(jnp.float32).reshape(*x.shape[:-1], A_ROPE // 2, 2)
    x1, x2 = xr[..., 0], xr[..., 1]
    c, s = cos[:, None, :], sin[:, None, :]
    out = jnp.stack([x1 * c - x2 * s, x1 * s + x2 * c], axis=-1)
    return out.reshape(x.shape).astype(x.dtype)


def _mlstm_kvg(z, p):
    B, T = z['m_k'].shape[:2]
    k = jax.nn.silu(_short_conv(z['m_k'], p['w_conv'][:, M_WIDTH:])).reshape(B, T, M_HEADS, M_HEAD_DIM)
    v = z['m_v'].reshape(B, T, M_HEADS, M_HEAD_DIM)
    g = (z['m_gate'] + p['b_gate']).reshape(B, T, 2, 2, M_HEADS)
    return k, v, g


def _mlstm_q(z, p):
    B, T = z['m_q'].shape[:2]
    q = jax.nn.silu(_short_conv(z['m_q'], p['w_conv'][:, :M_WIDTH])) * (M_HEAD_DIM ** -0.5)
    return q.reshape(B, T, M_HEADS, M_HEAD_DIM)


def _mlstm_scan(k, v, ig, fg, state, q=None):
    B, T, H, dh = k.shape
    nc = T // M_CHUNK

    def chunks(a):
        a = a.astype(jnp.float32).reshape(B, nc, M_CHUNK, H, *a.shape[3:])
        return jnp.moveaxis(a, (1, 3), (0, 2))

    xs = (chunks(k), chunks(v), chunks(ig), chunks(jax.nn.log_sigmoid(fg.astype(jnp.float32))))
    if q is not None:
        xs = xs + (chunks(q),)
    tri = jnp.tril(jnp.ones((M_CHUNK, M_CHUNK), dtype=bool))

    def body(carry, xc):
        C, n, m = carry
        kc, vc, ic, lc = xc[:4]
        b = jnp.cumsum(lc, axis=-1)
        bL = b[..., -1]
        end_log = bL[..., None] - b + ic
        m_new = jnp.maximum(bL + m, jnp.max(end_log, axis=-1))
        a_state = jnp.exp(bL + m - m_new)
        w_end = jnp.exp(end_log - m_new[..., None])
        C_new = a_state[..., None, None] * C + jnp.einsum('bhs,bhsk,bhsv->bhkv', w_end, kc, vc)
        n_new = a_state[..., None] * n + jnp.einsum('bhs,bhsk->bhk', w_end, kc)
        new = (C_new, n_new, m_new)
        if q is None:
            return new, None
        qc = xc[4]
        dlog = jnp.where(tri, b[..., :, None] - b[..., None, :] + ic[..., None, :], -jnp.inf)
        inter = b + m[..., None]
        mj = jnp.maximum(inter, jnp.max(dlog, axis=-1))
        s = jnp.einsum('bhjd,bhsd->bhjs', qc, kc) * jnp.exp(dlog - mj[..., None])
        e_inter = jnp.exp(inter - mj)
        num = s @ vc + e_inter[..., None] * (qc @ C)
        den = jnp.sum(s, axis=-1) + e_inter * jnp.einsum('bhjk,bhk->bhj', qc, n)
        h = num / jnp.maximum(jnp.abs(den), jnp.exp(-mj))[..., None]
        return new, h

    state, hs = lax.scan(body, state, xs)
    if q is None:
        return state, None
    return state, jnp.moveaxis(hs, (0, 2), (1, 3)).reshape(B, T, H, dh)


def _identity(a):
    return a


def _flip(a):
    return a[:, ::-1]


def _mlstm_out(h, z, p):
    B, T = h.shape[:2]
    hn = _rmsnorm(h, p['g_mh'].reshape(M_HEADS, M_HEAD_DIM)).astype(z['m_o'].dtype).reshape(B, T, M_WIDTH)
    return jax.nn.sigmoid(z['m_o']) * hn


def _mla_kv(z, p, cos, sin):
    B, T = z['a_ckv'].shape[:2]
    kv = (_rmsnorm(z['a_ckv'], p['g_kva']) @ p['w_ukv']).reshape(B, T, A_HEADS, A_NOPE + A_VDIM)
    k_nope, v = kv[..., :A_NOPE], kv[..., A_NOPE:]
    k_rope = z['a_kr'][:, :, None, :]
    if cos is not None:
        k_rope = _rope(k_rope, cos, sin)
    k = jnp.concatenate([k_nope, jnp.broadcast_to(k_rope, (B, T, A_HEADS, A_ROPE))], axis=-1)
    return k, v


def _mla_q(z, p, cos, sin):
    B, T = z['a_cq'].shape[:2]
    q = (_rmsnorm(z['a_cq'], p['g_qa']) @ p['w_uq']).reshape(B, T, A_HEADS, A_NOPE + A_ROPE)
    q_nope, q_rope = q[..., :A_NOPE], q[..., A_NOPE:]
    if cos is not None:
        q_rope = _rope(q_rope, cos, sin)
    return jnp.concatenate([q_nope, q_rope], axis=-1)


def _attend(q, k, v):
    B, T, H, dqk = q.shape
    blk = min(A_QBLOCK, T)
    nb = T // blk
    scale = dqk ** -0.5
    qb = jnp.moveaxis(q.reshape(B, nb, blk, H, dqk), 1, 0)

    def one(qi):
        s = jnp.einsum('bqhd,bkhd->bhqk', qi, k).astype(jnp.float32) * scale
        pr = jax.nn.softmax(s, axis=-1).astype(v.dtype)
        return jnp.einsum('bhqk,bkhd->bqhd', pr, v)

    o = lax.map(one, qb)
    return jnp.moveaxis(o, 0, 1).reshape(B, T, H * v.shape[-1])


def _merge(hm, ha, z, p):
    g_m, g_a = jnp.split(jax.nn.sigmoid(z['br_gate']), N_BRANCH, axis=-1)
    return (g_m * (hm @ p['w_bm']) + g_a * (ha @ p['w_ba'])) @ p['w_out']


def _token_mix(hc, hx, p, cos, sin, ctx_out):
    zc = _project(hc, p['w_in'], ALL_GROUPS if ctx_out else CTX_KV_GROUPS)
    zx = _project(hx, p['w_in'], ALL_GROUPS)
    B = hx.shape[0]
    kc, vc, gc = _mlstm_kvg(zc, p)
    kx, vx, gx = _mlstm_kvg(zx, p)
    qc = _mlstm_q(zc, p) if ctx_out else None
    qx = _mlstm_q(zx, p)
    state0 = (jnp.zeros((B, M_HEADS, M_HEAD_DIM, M_HEAD_DIM), jnp.float32), jnp.zeros((B, M_HEADS, M_HEAD_DIM), jnp.float32), jnp.zeros((B, M_HEADS), jnp.float32))
    h_ctx, h_lat = 0.0, 0.0
    for d in range(2):
        f = _identity if d == 0 else _flip
        st_c, hcd = _mlstm_scan(f(kc), f(vc), f(gc[:, :, d, 0]), f(gc[:, :, d, 1]), state0, None if qc is None else f(qc))
        _, hxd = _mlstm_scan(f(kx), f(vx), f(gx[:, :, d, 0]), f(gx[:, :, d, 1]), st_c, f(qx))
        h_lat = h_lat + f(hxd)
        if ctx_out:
            h_ctx = h_ctx + f(hcd)
    k_c, v_c = _mla_kv(zc, p, None, None)
    k_x, v_x = _mla_kv(zx, p, cos, sin)
    a_lat = _attend(_mla_q(zx, p, cos, sin), jnp.concatenate([k_c, k_x], axis=1), jnp.concatenate([v_c, v_x], axis=1))
    y_lat = _merge(_mlstm_out(h_lat, zx, p), a_lat, zx, p)
    if not ctx_out:
        return None, y_lat
    a_ctx = _attend(_mla_q(zc, p, None, None), k_c, v_c)
    y_ctx = _merge(_mlstm_out(h_ctx, zc, p), a_ctx, zc, p)
    return y_ctx, y_lat


def setup_inputs(seed: int = 0) -> dict:
    key = jax.random.key(seed)
    ks = jax.random.split(key, 32)
    L, D = DEPTH, D_MODEL

    def nrm(k, shape, s):
        return s * jax.random.normal(k, shape, jnp.float32)

    def gain(k, shape):
        return 1.0 + 0.05 * jax.random.normal(k, shape, jnp.float32)

    ib = nrm(ks[20], (L, 2, M_HEADS), 0.1)
    fb = 3.0 + 3.0 * jax.random.uniform(ks[21], (L, 2, M_HEADS), jnp.float32)
    b_gate = jnp.stack([ib, fb], axis=2).reshape(L, 4 * M_HEADS)
    return {
        'x': nrm(ks[0], (BATCH, SEQ, D), 1.0),
        'c': nrm(ks[1], (BATCH, D), 1.0),
        'ctx': nrm(ks[2], (BATCH, CTX_LEN, D), 1.0),
        'c_ctx': nrm(ks[3], (D,), 1.0),
        'w_ada': nrm(ks[4], (L, D, N_MOD * D), 0.5 * D ** -0.5),
        'b_ada': nrm(ks[5], (L, N_MOD * D), 0.02),
        'g_n1': gain(ks[6], (L, D)),
        'g_n2': gain(ks[7], (L, D)),
        'g_n3': gain(ks[8], (L, D)),
        'w_ff1_up': nrm(ks[9], (L, D, 2 * D_FF), D ** -0.5),
        'w_ff1_dn': nrm(ks[10], (L, D_FF, D), D_FF ** -0.5),
        'w_ff2_up': nrm(ks[11], (L, D, 2 * D_FF), D ** -0.5),
        'w_ff2_dn': nrm(ks[12], (L, D_FF, D), D_FF ** -0.5),
        'w_in': nrm(ks[13], (L, D, IN_WIDTH), D ** -0.5),
        'b_gate': b_gate,
        'w_conv': nrm(ks[14], (L, M_CONV, 2 * M_WIDTH), M_CONV ** -0.5),
        'g_mh': gain(ks[15], (L, M_WIDTH)),
        'g_qa': gain(ks[16], (L, A_QRANK)),
        'g_kva': gain(ks[17], (L, A_KVRANK)),
        'w_uq': nrm(ks[18], (L, A_QRANK, A_HEADS * (A_NOPE + A_ROPE)), A_QRANK ** -0.5),
        'w_ukv': nrm(ks[19], (L, A_KVRANK, A_HEADS * (A_NOPE + A_VDIM)), A_KVRANK ** -0.5),
        'w_bm': nrm(ks[22], (L, M_WIDTH, D), M_WIDTH ** -0.5),
        'w_ba': nrm(ks[23], (L, A_HEADS * A_VDIM, D), (A_HEADS * A_VDIM) ** -0.5),
        'w_out': nrm(ks[24], (L, D, D), D ** -0.5),
        'g_final': gain(ks[25], (D,)),
    }


def reference(x, c, ctx, c_ctx, w_ada, b_ada, g_n1, g_n2, g_n3, w_ff1_up, w_ff1_dn, w_ff2_up, w_ff2_dn, w_in, b_gate, w_conv, g_mh, g_qa, g_kva, w_uq, w_ukv, w_bm, w_ba, w_out, g_final):
    S = x.shape[1]
    ROWS = S // GRID_W
    row = jnp.repeat(jnp.arange(ROWS, dtype=jnp.float32), GRID_W)
    col = jnp.tile(jnp.arange(GRID_W, dtype=jnp.float32), ROWS)
    cos, sin = _rope_tables(row, col)
    for l in range(DEPTH):
        last = l == DEPTH - 1
        p = {'w_in': w_in[l], 'b_gate': b_gate[l], 'w_conv': w_conv[l], 'g_mh': g_mh[l], 'g_qa': g_qa[l], 'g_kva': g_kva[l], 'w_uq': w_uq[l], 'w_ukv': w_ukv[l], 'w_bm': w_bm[l], 'w_ba': w_ba[l], 'w_out': w_out[l]}
        mx = [m[:, None, :] for m in jnp.split(jax.nn.silu(c) @ w_ada[l] + b_ada[l], N_MOD, axis=-1)]
        mc = jnp.split(jax.nn.silu(c_ctx) @ w_ada[l] + b_ada[l], N_MOD, axis=-1)
        x = x + 0.5 * mx[2] * _swiglu(_modulate(_rmsnorm(x, g_n1[l]), mx[0], mx[1]), w_ff1_up[l], w_ff1_dn[l])
        ctx = ctx + 0.5 * mc[2] * _swiglu(_modulate(_rmsnorm(ctx, g_n1[l]), mc[0], mc[1]), w_ff1_up[l], w_ff1_dn[l])
        y_ctx, y_lat = _token_mix(_modulate(_rmsnorm(ctx, g_n2[l]), mc[3], mc[4]), _modulate(_rmsnorm(x, g_n2[l]), mx[3], mx[4]), p, cos, sin, not last)
        x = x + mx[5] * y_lat
        x = x + 0.5 * mx[8] * _swiglu(_modulate(_rmsnorm(x, g_n3[l]), mx[6], mx[7]), w_ff2_up[l], w_ff2_dn[l])
        if not last:
            ctx = ctx + mc[5] * y_ctx
            ctx = ctx + 0.5 * mc[8] * _swiglu(_modulate(_rmsnorm(ctx, g_n3[l]), mc[6], mc[7]), w_ff2_up[l], w_ff2_dn[l])
    return _rmsnorm(x, g_final)
```

```python
import functools

import jax
import jax.numpy as jnp
from jax import lax
from jax.experimental import pallas as pl
from jax.experimental.pallas import tpu as pltpu

F32 = jnp.float32
BF16 = jnp.bfloat16
EPS = 1e-6

GRID_W = 64
ROPE_THETA = 10000.0
M_HEADS = 4
M_HEAD_DIM = 256
M_CONV = 3
A_HEADS = 8
A_NOPE = 128
A_ROPE = 64
A_VDIM = 128
A_QRANK = 384
A_KVRANK = 256
N_MOD = 9

TOKEN_TILE = 256
HALO = 16
ATTN_KV_TILE = 512
LANE = 128
A_QK = 2 * LANE
ZF_WIDTH = A_QRANK + A_KVRANK + LANE + LANE
ZF_KR = A_QRANK + A_KVRANK
ZF_GATE = ZF_KR + LANE
VMEM_LIMIT = 48 * 1024 * 1024


def _dot(a, b):
    return jnp.dot(a, b, preferred_element_type=F32)


def _rms(x, g):
    return x * lax.rsqrt(jnp.mean(x * x, axis=-1, keepdims=True) + EPS) * g


def _norm_mod(x, g, shift, scale):
    return _rms(x, g) * (1 + scale) + shift


def _params(n_axes, vmem=VMEM_LIMIT):
    return pltpu.CompilerParams(dimension_semantics=("arbitrary",) * n_axes, vmem_limit_bytes=vmem)


def _resident(block, index_map):
    return pl.BlockSpec(block, index_map, pipeline_mode=pl.Buffered(1))


def _mod_kernel(c_ref, w_ref, b_ref, o_ref):
    cv = c_ref[...]
    s = (cv * jax.nn.sigmoid(cv)).astype(BF16)
    o_ref[...] = _dot(s, w_ref[...].astype(BF16)) + b_ref[...]


def _modulations(cvec, w_ada, b_ada):
    L, D, N = w_ada.shape
    tn = 1024
    return pl.pallas_call(
        _mod_kernel,
        grid=(L, N // tn),
        in_specs=[
            pl.BlockSpec((8, D), lambda l, j: (0, 0)),
            pl.BlockSpec((None, D, tn), lambda l, j: (l, 0, j)),
            pl.BlockSpec((None, 1, tn), lambda l, j: (l, 0, j)),
        ],
        out_specs=pl.BlockSpec((None, 8, tn), lambda l, j: (l, 0, j)),
        out_shape=jax.ShapeDtypeStruct((L, 8, N), F32),
        compiler_params=_params(2),
        name="adaln_mod",
    )(cvec, w_ada, b_ada.reshape(L, 1, N))


def _ffn_kernel(r_ref, mod_ref, g_ref, wup_ref, wdn_ref, o_ref, *, d_ff, n_chunks):
    x = r_ref[...]
    mod = mod_ref[...]
    h = _norm_mod(x, g_ref[...], mod[0:1], mod[1:2]).astype(BF16)
    tf = d_ff // n_chunks
    acc = None
    for ci in range(n_chunks):
        a = _dot(h, wup_ref[:, ci * tf:(ci + 1) * tf])
        b = _dot(h, wup_ref[:, d_ff + ci * tf:d_ff + (ci + 1) * tf])
        act = (a * jax.nn.sigmoid(a) * b).astype(BF16)
        part = _dot(act, wdn_ref[ci * tf:(ci + 1) * tf, :])
        acc = part if acc is None else acc + part
    o_ref[...] = x + (0.5 * mod[2:3]) * acc


def _ffn(R, mods, g, w_up, w_dn, l, sub, n_x_tiles):
    B, T, D = R.shape
    d_ff = w_dn.shape[1]
    nt = T // TOKEN_TILE
    who = lambda b, t: jnp.where(t >= n_x_tiles, B, b)
    return pl.pallas_call(
        functools.partial(_ffn_kernel, d_ff=d_ff, n_chunks=2),
        grid=(B, nt),
        in_specs=[
            pl.BlockSpec((None, TOKEN_TILE, D), lambda b, t: (b, t, 0)),
            pl.BlockSpec((None, None, None, 3, D), lambda b, t: (l, who(b, t), sub, 0, 0)),
            pl.BlockSpec((None, 1, D), lambda b, t: (l, 0, 0)),
            _resident((None, D, 2 * d_ff), lambda b, t: (l, 0, 0)),
            _resident((None, d_ff, D), lambda b, t: (l, 0, 0)),
        ],
        out_specs=pl.BlockSpec((None, TOKEN_TILE, D), lambda b, t: (b, t, 0)),
        out_shape=jax.ShapeDtypeStruct(R.shape, F32),
        compiler_params=_params(2),
        name="ffn_half_step",
    )(R, mods, g, w_up, w_dn)


def _proj_kernel(r_ref, rp_ref, rn_ref, mod_ref, g_ref, wqk_ref, wb_ref, wf_ref, bf_ref, wc_ref,
                 qk_ref, zb_ref, zf_ref, hfull_ref, *, n_x_tiles, n_tiles):
    mod = mod_ref[...]
    g = g_ref[...]
    nm = lambda x: _norm_mod(x, g, mod[0:1], mod[1:2])
    t = pl.program_id(1)
    has_prev = jnp.logical_and(t != 0, t != n_x_tiles)
    has_next = jnp.logical_and(t != n_x_tiles - 1, t != n_tiles - 1)
    hm = nm(r_ref[...]).astype(BF16)
    hfull_ref[0:HALO, :] = jnp.where(has_prev, nm(rp_ref[...]), 0.0).astype(BF16)
    hfull_ref[HALO:HALO + TOKEN_TILE, :] = hm
    hfull_ref[HALO + TOKEN_TILE:, :] = jnp.where(has_next, nm(rn_ref[...]), 0.0).astype(BF16)
    hfull = hfull_ref[...]
    rows = TOKEN_TILE + 2 * HALO
    width = wqk_ref.shape[1]
    cw = 512
    for ci in range(width // cw):
        cs = slice(ci * cw, (ci + 1) * cw)
        z = _dot(hfull, wqk_ref[:, cs])
        w = wc_ref[:, cs]
        cv = w[0:1] * pltpu.roll(z, 1, 0) + w[1:2] * z + w[2:3] * pltpu.roll(z, rows - 1, 0)
        cv = cv[HALO:HALO + TOKEN_TILE]
        y = cv * jax.nn.sigmoid(cv)
        if ci * cw < width // 2:
            y = y * (M_HEAD_DIM ** -0.5)
        qk_ref[:, cs] = y.astype(BF16)
    zb_ref[...] = _dot(hm, wb_ref[...]).astype(BF16)
    zf_ref[...] = _dot(hm, wf_ref[...]) + bf_ref[...]


def _project(R, mods, g, wqk, wb, wf, bf, wconv, l, n_x_tiles):
    B, T, D = R.shape
    nt = T // TOKEN_TILE
    hb = TOKEN_TILE // HALO
    n_halo_blocks = T // HALO
    who = lambda b, t: jnp.where(t >= n_x_tiles, B, b)
    nqk, nb, nf = wqk.shape[2], wb.shape[2], wf.shape[2]
    return pl.pallas_call(
        functools.partial(_proj_kernel, n_x_tiles=n_x_tiles, n_tiles=nt),
        grid=(B, nt),
        in_specs=[
            pl.BlockSpec((None, TOKEN_TILE, D), lambda b, t: (b, t, 0)),
            pl.BlockSpec((None, HALO, D), lambda b, t: (b, jnp.maximum(t * hb - 1, 0), 0)),
            pl.BlockSpec((None, HALO, D), lambda b, t: (b, jnp.minimum((t + 1) * hb, n_halo_blocks - 1), 0)),
            pl.BlockSpec((None, None, None, 3, D), lambda b, t: (l, who(b, t), 1, 0, 0)),
            pl.BlockSpec((None, 1, D), lambda b, t: (l, 0, 0)),
            _resident((None, D, nqk), lambda b, t: (l, 0, 0)),
            _resident((None, D, nb), lambda b, t: (l, 0, 0)),
            _resident((None, D, nf), lambda b, t: (l, 0, 0)),
            pl.BlockSpec((None, 1, nf), lambda b, t: (l, 0, 0)),
            pl.BlockSpec((None, M_CONV, nqk), lambda b, t: (l, 0, 0)),
        ],
        out_specs=[
            pl.BlockSpec((None, TOKEN_TILE, nqk), lambda b, t: (b, t, 0)),
            pl.BlockSpec((None, TOKEN_TILE, nb), lambda b, t: (b, t, 0)),
            pl.BlockSpec((None, TOKEN_TILE, nf), lambda b, t: (b, t, 0)),
        ],
        out_shape=[
            jax.ShapeDtypeStruct((B, T, nqk), BF16),
            jax.ShapeDtypeStruct((B, T, nb), BF16),
            jax.ShapeDtypeStruct((B, T, nf), F32),
        ],
        scratch_shapes=[pltpu.VMEM((TOKEN_TILE + 2 * HALO, D), BF16)],
        compiler_params=_params(2),
        name="in_projection",
    )(R, R, R, mods, g, wqk, wb, wf, bf, wconv)


def _mlstm_kernel(q_ref, k_ref, v_ref, gc_ref, gr_ref, o_ref, c_ref, *, n_x, seq):
    L = TOKEN_TILE
    dh = q_ref.shape[1]
    row = lax.broadcasted_iota(jnp.int32, (L, L), 0)
    col = lax.broadcasted_iota(jnp.int32, (L, L), 1)
    lower = col <= row
    upper = col >= row

    def step(d, c0, n, m, accumulate):
        tri, tri_t = (lower, upper) if d == 0 else (upper, lower)
        sl = pl.ds(c0, L)
        q, k, v = q_ref[sl, :], k_ref[sl, :], v_ref[sl, :]
        gr = gr_ref[:, sl]
        gc = gc_ref[sl, :]
        ig_row, lf_row = gr[2 * d:2 * d + 1], jax.nn.log_sigmoid(gr[2 * d + 1:2 * d + 2])
        ig_col, lf_col = gc[:, 2 * d:2 * d + 1], jax.nn.log_sigmoid(gc[:, 2 * d + 1:2 * d + 2])
        b_col = jnp.sum(jnp.where(tri, lf_row, 0.0), axis=1, keepdims=True)
        b_row = jnp.sum(jnp.where(tri_t, lf_col, 0.0), axis=0, keepdims=True)
        b_end = jnp.sum(lf_row, axis=1, keepdims=True)
        g_row = ig_row - b_row
        m_new = jnp.maximum(b_end + m, b_end + jnp.max(g_row, axis=1, keepdims=True))
        a_state = jnp.exp(b_end + m - m_new)
        w_end = jnp.exp(b_end - b_col + ig_col - m_new)
        c_old = c_ref[d]
        inter = b_col + m
        mj = jnp.maximum(inter, b_col + jnp.max(jnp.where(tri, g_row, -jnp.inf), axis=1, keepdims=True))
        decay = jnp.exp(jnp.where(tri, b_col + g_row - mj, -jnp.inf))
        s = lax.dot_general(q, k, (((1,), (1,)), ((), ())), preferred_element_type=F32) * decay
        e_inter = jnp.exp(inter - mj)
        num = _dot(s.astype(BF16), v) + e_inter * _dot(q, c_old.astype(BF16))
        den = jnp.sum(s, axis=1, keepdims=True) + e_inter * jnp.sum(q.astype(F32) * n, axis=1, keepdims=True)
        hval = num * (1.0 / jnp.maximum(jnp.abs(den), jnp.exp(-mj)))
        if accumulate:
            o_ref[sl, :] += hval
        else:
            o_ref[sl, :] = hval
        kw = k.astype(F32) * w_end
        c_ref[d] = a_state * c_old + lax.dot_general(kw.astype(BF16), v, (((0,), (0,)), ((), ())),
                                                     preferred_element_type=F32)
        return a_state * n + jnp.sum(kw, axis=0, keepdims=True), m_new

    c_ref[...] = jnp.zeros(c_ref.shape, F32)
    n0 = jnp.zeros((1, dh), F32)
    m0 = jnp.zeros((1, 1), F32)
    nf, mf = step(0, seq, n0, m0, False)
    nb, mb = step(1, seq, n0, m0, True)

    def both(accumulate):
        def body(i, carry):
            nf, mf, nb, mb = carry
            nf, mf = step(0, pl.multiple_of(i * L, L), nf, mf, accumulate)
            nb, mb = step(1, pl.multiple_of((n_x - 1 - i) * L, L), nb, mb, accumulate)
            return nf, mf, nb, mb
        return body

    carry = lax.fori_loop(0, n_x // 2, both(False), (nf, mf, nb, mb))
    lax.fori_loop(n_x // 2, n_x, both(True), carry)


def _mlstm(QK, ZB, gcol, grow, seq):
    B, T, _ = QK.shape
    dh = M_HEAD_DIM
    n_x = seq // TOKEN_TILE
    return pl.pallas_call(
        functools.partial(_mlstm_kernel, n_x=n_x, seq=seq),
        grid=(B, M_HEADS),
        in_specs=[
            pl.BlockSpec((None, T, dh), lambda b, h: (b, 0, h)),
            pl.BlockSpec((None, T, dh), lambda b, h: (b, 0, M_HEADS + h)),
            pl.BlockSpec((None, T, dh), lambda b, h: (b, 0, h)),
            pl.BlockSpec((None, None, T, 4), lambda b, h: (b, h, 0, 0)),
            pl.BlockSpec((None, None, 4, T), lambda b, h: (b, h, 0, 0)),
        ],
        out_specs=pl.BlockSpec((None, T, dh), lambda b, h: (b, 0, h)),
        out_shape=jax.ShapeDtypeStruct((B, T, M_HEADS * dh), F32),
        scratch_shapes=[pltpu.VMEM((2, dh, dh), F32)],
        compiler_params=_params(2),
        name="mlstm_scan",
    )(QK, QK, ZB, gcol, grow)


def _mla_prep_kernel(zf_ref, cos_ref, sa_ref, sb_ref, gq_ref, gkv_ref, wuq_ref, wuk_ref, wuv_ref,
                     aq_ref, akt_ref, av_ref, *, scale):
    zf = zf_ref[...]
    cqn = _rms(zf[:, :A_QRANK], gq_ref[...]).astype(BF16)
    ckvn = _rms(zf[:, A_QRANK:ZF_KR], gkv_ref[...]).astype(BF16)
    cos, sa, sb = cos_ref[...], sa_ref[...], sb_ref[...]

    def rope(x):
        return x * cos + pltpu.roll(x, A_ROPE // 2, 1) * sa + pltpu.roll(x, LANE - A_ROPE // 2, 1) * sb

    qf = _dot(cqn, wuq_ref[...])
    for h in range(A_HEADS):
        o = h * A_QK
        aq_ref[:, o:o + A_NOPE] = (qf[:, o:o + A_NOPE] * scale).astype(BF16)
        aq_ref[:, o + A_NOPE:o + A_QK] = (rope(qf[:, o + A_NOPE:o + A_QK]) * scale).astype(BF16)
    kn = _dot(ckvn, wuk_ref[...])
    kr = rope(zf[:, ZF_KR:ZF_KR + LANE])
    for h in range(A_HEADS):
        kh = jnp.concatenate([kn[:, h * A_NOPE:(h + 1) * A_NOPE], kr], axis=1)
        akt_ref[h * A_QK:(h + 1) * A_QK, :] = kh.T.astype(BF16)
    av_ref[...] = _dot(ckvn, wuv_ref[...]).astype(BF16)


def _mla_prep(ZF, cos_t, sin_a, sin_b, gq, gkv, wuq, wuk, wuv, l):
    B, T, _ = ZF.shape
    nt = T // TOKEN_TILE
    scale = (A_NOPE + A_ROPE) ** -0.5
    return pl.pallas_call(
        functools.partial(_mla_prep_kernel, scale=scale),
        grid=(B, nt),
        in_specs=[
            pl.BlockSpec((None, TOKEN_TILE, ZF_WIDTH), lambda b, t: (b, t, 0)),
            pl.BlockSpec((TOKEN_TILE, LANE), lambda b, t: (t, 0)),
            pl.BlockSpec((TOKEN_TILE, LANE), lambda b, t: (t, 0)),
            pl.BlockSpec((TOKEN_TILE, LANE), lambda b, t: (t, 0)),
            pl.BlockSpec((None, 1, A_QRANK), lambda b, t: (l, 0, 0)),
            pl.BlockSpec((None, 1, A_KVRANK), lambda b, t: (l, 0, 0)),
            _resident((None, A_QRANK, A_HEADS * A_QK), lambda b, t: (l, 0, 0)),
            _resident((None, A_KVRANK, A_HEADS * A_NOPE), lambda b, t: (l, 0, 0)),
            _resident((None, A_KVRANK, A_HEADS * A_VDIM), lambda b, t: (l, 0, 0)),
        ],
        out_specs=[
            pl.BlockSpec((None, TOKEN_TILE, A_HEADS * A_QK), lambda b, t: (b, t, 0)),
            pl.BlockSpec((None, A_HEADS * A_QK, TOKEN_TILE), lambda b, t: (b, 0, t)),
            pl.BlockSpec((None, TOKEN_TILE, A_HEADS * A_VDIM), lambda b, t: (b, t, 0)),
        ],
        out_shape=[
            jax.ShapeDtypeStruct((B, T, A_HEADS * A_QK), BF16),
            jax.ShapeDtypeStruct((B, A_HEADS * A_QK, T), BF16),
            jax.ShapeDtypeStruct((B, T, A_HEADS * A_VDIM), BF16),
        ],
        compiler_params=_params(2),
        name="mla_up_projection",
    )(ZF, cos_t, sin_a, sin_b, gq, gkv, wuq, wuk, wuv)


def _attn_kernel(q_ref, kt_ref, v_ref, o_ref, *, seq, ctx_len, n_x_tiles):
    q = q_ref[...]
    tq = q.shape[0]
    tk = ATTN_KV_TILE

    def update(carry, kt, v):
        m, l, acc = carry
        s = _dot(q, kt)
        m_new = jnp.maximum(m, jnp.max(s, axis=1, keepdims=True))
        alpha = jnp.exp(m - m_new)
        p = jnp.exp(s - m_new)
        return m_new, alpha * l + jnp.sum(p, axis=1, keepdims=True), alpha * acc + _dot(p.astype(BF16), v)

    init = (jnp.full((tq, 1), -jnp.inf, F32), jnp.zeros((tq, 1), F32), jnp.zeros((tq, A_VDIM), F32))
    carry = update(init, kt_ref[:, seq:seq + ctx_len], v_ref[seq:seq + ctx_len, :])

    def body(i, carry):
        off = pl.multiple_of(i * tk, tk)
        return update(carry, kt_ref[:, pl.ds(off, tk)], v_ref[pl.ds(off, tk), :])

    n_latent_steps = jnp.where(pl.program_id(2) < n_x_tiles, seq // tk, 0)
    _, l, acc = lax.fori_loop(0, n_latent_steps, body, carry)
    o_ref[...] = (acc * (1.0 / l)).astype(o_ref.dtype)


def _attention(AQ, AKT, AV, seq):
    B, T, _ = AQ.shape
    nt = T // TOKEN_TILE
    return pl.pallas_call(
        functools.partial(_attn_kernel, seq=seq, ctx_len=T - seq, n_x_tiles=seq // TOKEN_TILE),
        grid=(B, A_HEADS, nt),
        in_specs=[
            pl.BlockSpec((None, TOKEN_TILE, A_QK), lambda b, h, t: (b, t, h)),
            pl.BlockSpec((None, A_QK, T), lambda b, h, t: (b, h, 0)),
            pl.BlockSpec((None, T, A_VDIM), lambda b, h, t: (b, 0, h)),
        ],
        out_specs=pl.BlockSpec((None, TOKEN_TILE, A_VDIM), lambda b, h, t: (b, t, h)),
        out_shape=jax.ShapeDtypeStruct((B, T, A_HEADS * A_VDIM), BF16),
        compiler_params=_params(3),
        name="mla_attention",
    )(AQ, AKT, AV)


def _merge_kernel(h_ref, mo_ref, bg_ref, a_ref, r_ref, mod_ref, gmh_ref, wbm_ref, wba_ref, wout_ref, o_ref):
    hh = h_ref[...]
    dh = M_HEAD_DIM
    hn = jnp.concatenate(
        [hh[:, i * dh:(i + 1) * dh]
         * lax.rsqrt(jnp.mean(hh[:, i * dh:(i + 1) * dh] * hh[:, i * dh:(i + 1) * dh], axis=-1, keepdims=True) + EPS)
         for i in range(M_HEADS)], axis=1) * gmh_ref[...]
    hm = (jax.nn.sigmoid(mo_ref[...].astype(F32)) * hn).astype(BF16)
    width = hh.shape[1]
    bg = bg_ref[...].astype(F32)
    u = jax.nn.sigmoid(bg[:, :width]) * _dot(hm, wbm_ref[...]) + jax.nn.sigmoid(bg[:, width:]) * _dot(a_ref[...], wba_ref[...])
    y = _dot(u.astype(BF16), wout_ref[...])
    o_ref[...] = r_ref[...] + mod_ref[...][2:3] * y


def _merge(H, ZB, A, R, mods, gmh, wbm, wba, wout, l, n_x_tiles):
    B, T, D = R.shape
    nt = T // TOKEN_TILE
    W = H.shape[2]
    who = lambda b, t: jnp.where(t >= n_x_tiles, B, b)
    return pl.pallas_call(
        _merge_kernel,
        grid=(B, nt),
        in_specs=[
            pl.BlockSpec((None, TOKEN_TILE, W), lambda b, t: (b, t, 0)),
            pl.BlockSpec((None, TOKEN_TILE, W), lambda b, t: (b, t, 1)),
            pl.BlockSpec((None, TOKEN_TILE, 2 * D), lambda b, t: (b, t, 1)),
            pl.BlockSpec((None, TOKEN_TILE, A.shape[2]), lambda b, t: (b, t, 0)),
            pl.BlockSpec((None, TOKEN_TILE, D), lambda b, t: (b, t, 0)),
            pl.BlockSpec((None, None, None, 3, D), lambda b, t: (l, who(b, t), 1, 0, 0)),
            pl.BlockSpec((None, 1, W), lambda b, t: (l, 0, 0)),
            _resident((None, W, D), lambda b, t: (l, 0, 0)),
            _resident((None, A.shape[2], D), lambda b, t: (l, 0, 0)),
            _resident((None, D, D), lambda b, t: (l, 0, 0)),
        ],
        out_specs=pl.BlockSpec((None, TOKEN_TILE, D), lambda b, t: (b, t, 0)),
        out_shape=jax.ShapeDtypeStruct(R.shape, F32),
        compiler_params=_params(2),
        name="mixer_merge",
    )(H, ZB, ZB, A, R, mods, gmh, wbm, wba, wout)


def _final_kernel(r_ref, g_ref, o_ref):
    o_ref[...] = _rms(r_ref[...], g_ref[...])


def _final_norm(R, g, seq):
    B, T, D = R.shape
    return pl.pallas_call(
        _final_kernel,
        grid=(B, seq // TOKEN_TILE),
        in_specs=[
            pl.BlockSpec((None, TOKEN_TILE, D), lambda b, t: (b, t, 0)),
            pl.BlockSpec((1, D), lambda b, t: (0, 0)),
        ],
        out_specs=pl.BlockSpec((None, TOKEN_TILE, D), lambda b, t: (b, t, 0)),
        out_shape=jax.ShapeDtypeStruct((B, seq, D), F32),
        compiler_params=_params(2),
        name="final_norm",
    )(R, g.reshape(1, D))


def _rope_tables(seq, ctx_len):
    half = A_ROPE // 2
    axis_rot = A_ROPE // 2
    pos = jnp.arange(seq)
    rowp = (pos // GRID_W).astype(F32)
    colp = (pos % GRID_W).astype(F32)
    inv = ROPE_THETA ** (-jnp.arange(0, axis_rot, 2, dtype=F32) / axis_rot)
    ang = jnp.concatenate([rowp[:, None] * inv, colp[:, None] * inv], axis=-1)
    cos, sin = jnp.cos(ang), jnp.sin(ang)
    zeros = jnp.zeros((seq, half), F32)
    pad = jnp.zeros((seq, LANE - A_ROPE), F32)
    cos_x = jnp.concatenate([cos, cos, pad], axis=1)
    sa_x = jnp.concatenate([zeros, sin, pad], axis=1)
    sb_x = jnp.concatenate([-sin, zeros, pad], axis=1)
    cos_c = jnp.concatenate([jnp.ones((ctx_len, A_ROPE), F32), jnp.zeros((ctx_len, LANE - A_ROPE), F32)], axis=1)
    zc = jnp.zeros((ctx_len, LANE), F32)
    return (jnp.concatenate([cos_x, cos_c], axis=0), jnp.concatenate([sa_x, zc], axis=0),
            jnp.concatenate([sb_x, zc], axis=0))


def kernel(x, c, ctx, c_ctx, w_ada, b_ada, g_n1, g_n2, g_n3, w_ff1_up, w_ff1_dn, w_ff2_up, w_ff2_dn, w_in, b_gate, w_conv, g_mh, g_qa, g_kva, w_uq, w_ukv, w_bm, w_ba, w_out, g_final):
    B, S, D = x.shape
    C = ctx.shape[1]
    L = w_ada.shape[0]
    T = S + C
    MW = M_HEADS * M_HEAD_DIM
    assert C == TOKEN_TILE and S % (2 * TOKEN_TILE) == 0 and S % ATTN_KV_TILE == 0 and S % GRID_W == 0
    assert B + 1 <= 8 and D == MW
    n_x_tiles = S // TOKEN_TILE

    perm = jnp.concatenate([jnp.arange(0, A_ROPE, 2), jnp.arange(1, A_ROPE, 2)])
    o_gate = 4 * MW
    o_cq = o_gate + 4 * M_HEADS
    o_ckv = o_cq + A_QRANK
    o_kr = o_ckv + A_KVRANK
    o_br = o_kr + A_ROPE
    wqk = w_in[:, :, :2 * MW].astype(BF16)
    wb = jnp.concatenate([w_in[:, :, 2 * MW:4 * MW], w_in[:, :, o_br:]], axis=2).astype(BF16)
    wf = jnp.concatenate([
        w_in[:, :, o_cq:o_kr], w_in[:, :, o_kr:o_br][:, :, perm], jnp.zeros((L, D, LANE - A_ROPE), F32),
        w_in[:, :, o_gate:o_cq], jnp.zeros((L, D, LANE - 4 * M_HEADS), F32)], axis=2).astype(BF16)
    bf = jnp.concatenate([jnp.zeros((L, ZF_GATE), F32), b_gate, jnp.zeros((L, LANE - 4 * M_HEADS), F32)],
                         axis=1).reshape(L, 1, ZF_WIDTH)
    dq = A_NOPE + A_ROPE
    wuq4 = w_uq.reshape(L, A_QRANK, A_HEADS, dq)
    wuq = jnp.concatenate([wuq4[..., :A_NOPE], wuq4[..., A_NOPE:][..., perm],
                           jnp.zeros((L, A_QRANK, A_HEADS, A_QK - dq), F32)], axis=-1)
    wuq = wuq.reshape(L, A_QRANK, A_HEADS * A_QK).astype(BF16)
    wukv4 = w_ukv.reshape(L, A_KVRANK, A_HEADS, A_NOPE + A_VDIM)
    wuk = wukv4[..., :A_NOPE].reshape(L, A_KVRANK, A_HEADS * A_NOPE).astype(BF16)
    wuv = wukv4[..., A_NOPE:].reshape(L, A_KVRANK, A_HEADS * A_VDIM).astype(BF16)
    wup1, wdn1 = w_ff1_up.astype(BF16), w_ff1_dn.astype(BF16)
    wup2, wdn2 = w_ff2_up.astype(BF16), w_ff2_dn.astype(BF16)
    wbm, wba, wout = w_bm.astype(BF16), w_ba.astype(BF16), w_out.astype(BF16)
    row3 = lambda a: a.reshape(L, 1, a.shape[1])
    gn1, gn2, gn3, gmh, gqa, gkva = map(row3, (g_n1, g_n2, g_n3, g_mh, g_qa, g_kva))
    cos_t, sin_a, sin_b = _rope_tables(S, C)

    cvec = jnp.concatenate([c, c_ctx[None, :], jnp.zeros((8 - B - 1, D), F32)], axis=0)
    mods = _modulations(cvec, w_ada, b_ada).reshape(L, 8, N_MOD // 3, 3, D)

    R = jnp.concatenate([x, ctx], axis=1)
    for l in range(L):
        R = _ffn(R, mods, gn1, wup1, wdn1, l, 0, n_x_tiles)
        QK, ZB, ZF = _project(R, mods, gn2, wqk, wb, wf, bf, w_conv, l, n_x_tiles)
        gates = ZF[:, :, ZF_GATE:ZF_GATE + 4 * M_HEADS].reshape(B, T, 4, M_HEADS)
        gcol = jnp.transpose(gates, (0, 3, 1, 2))
        grow = jnp.transpose(gates, (0, 3, 2, 1))
        H = _mlstm(QK, ZB, gcol, grow, S)
        AQ, AKT, AV = _mla_prep(ZF, cos_t, sin_a, sin_b, gqa, gkva, wuq, wuk, wuv, l)
        A = _attention(AQ, AKT, AV, S)
        R = _merge(H, ZB, A, R, mods, gmh, wbm, wba, wout, l, n_x_tiles)
        R = _ffn(R, mods, gn3, wup2, wdn2, l, 2, n_x_tiles)
    return _final_norm(R, g_final, S)
```

```python
import functools
import math

import jax
import jax.numpy as jnp
from jax import lax
from jax.experimental import pallas as pl
from jax.experimental.pallas import tpu as pltpu

F32 = jnp.float32
BF16 = jnp.bfloat16
EPS = 1e-6

GRID_W = 64
ROPE_THETA = 10000.0
M_HEADS = 4
M_HEAD_DIM = 256
M_CONV = 3
A_HEADS = 8
A_NOPE = 128
A_ROPE = 64
A_VDIM = 128
A_QRANK = 384
A_KVRANK = 256
N_MOD = 9

TOKEN_TILE = 256
HALO = 16
ATTN_Q_TILE = 512
LANE = 128
A_QK = 2 * LANE
ZF_WIDTH = A_QRANK + A_KVRANK + LANE + LANE
ZF_KR = A_QRANK + A_KVRANK
ZF_GATE = ZF_KR + LANE
VMEM_LIMIT = 48 * 1024 * 1024


def _dot(a, b):
    return jnp.dot(a, b, preferred_element_type=F32)


def _rms(x, g):
    return x * lax.rsqrt(jnp.mean(x * x, axis=-1, keepdims=True) + EPS) * g


def _norm_mod(x, g, shift, scale):
    return _rms(x, g) * (1 + scale) + shift


def _params(n_axes, vmem=VMEM_LIMIT):
    return pltpu.CompilerParams(dimension_semantics=("arbitrary",) * n_axes, vmem_limit_bytes=vmem)


def _resident(block, index_map):
    return pl.BlockSpec(block, index_map, pipeline_mode=pl.Buffered(1))


def _mod_kernel(c_ref, w_ref, b_ref, o_ref):
    cv = c_ref[...]
    s = (cv * jax.nn.sigmoid(cv)).astype(BF16)
    o_ref[...] = _dot(s, w_ref[...].astype(BF16)) + b_ref[...]


def _modulations(cvec, w_ada, b_ada):
    L, D, N = w_ada.shape
    tn = 1024
    return pl.pallas_call(
        _mod_kernel,
        grid=(L, N // tn),
        in_specs=[
            pl.BlockSpec((8, D), lambda l, j: (0, 0)),
            pl.BlockSpec((None, D, tn), lambda l, j: (l, 0, j)),
            pl.BlockSpec((None, 1, tn), lambda l, j: (l, 0, j)),
        ],
        out_specs=pl.BlockSpec((None, 8, tn), lambda l, j: (l, 0, j)),
        out_shape=jax.ShapeDtypeStruct((L, 8, N), F32),
        compiler_params=_params(2),
        name="adaln_mod",
    )(cvec, w_ada, b_ada.reshape(L, 1, N))


def _ffn_kernel(r_ref, mod_ref, g_ref, wup_ref, wdn_ref, o_ref, *, d_ff, n_chunks):
    x = r_ref[...]
    mod = mod_ref[...]
    h = _norm_mod(x, g_ref[...], mod[0:1], mod[1:2]).astype(BF16)
    tf = d_ff // n_chunks
    acc = None
    for ci in range(n_chunks):
        a = _dot(h, wup_ref[:, ci * tf:(ci + 1) * tf])
        b = _dot(h, wup_ref[:, d_ff + ci * tf:d_ff + (ci + 1) * tf])
        act = (a * jax.nn.sigmoid(a) * b).astype(BF16)
        part = _dot(act, wdn_ref[ci * tf:(ci + 1) * tf, :])
        acc = part if acc is None else acc + part
    o_ref[...] = x + (0.5 * mod[2:3]) * acc


def _ffn(R, mods, g, w_up, w_dn, l, sub, n_x_tiles):
    B, T, D = R.shape
    d_ff = w_dn.shape[1]
    nt = T // TOKEN_TILE
    who = lambda b, t: jnp.where(t >= n_x_tiles, B, b)
    return pl.pallas_call(
        functools.partial(_ffn_kernel, d_ff=d_ff, n_chunks=2),
        grid=(B, nt),
        in_specs=[
            pl.BlockSpec((None, TOKEN_TILE, D), lambda b, t: (b, t, 0)),
            pl.BlockSpec((None, None, None, 3, D), lambda b, t: (l, who(b, t), sub, 0, 0)),
            pl.BlockSpec((None, 1, D), lambda b, t: (l, 0, 0)),
            _resident((None, D, 2 * d_ff), lambda b, t: (l, 0, 0)),
            _resident((None, d_ff, D), lambda b, t: (l, 0, 0)),
        ],
        out_specs=pl.BlockSpec((None, TOKEN_TILE, D), lambda b, t: (b, t, 0)),
        out_shape=jax.ShapeDtypeStruct(R.shape, F32),
        compiler_params=_params(2),
        name="ffn_half_step",
    )(R, mods, g, w_up, w_dn)


def _proj_kernel(r_ref, rp_ref, rn_ref, mod_ref, g_ref, wqk_ref, wb_ref, wf_ref, bf_ref, wc_ref,
                 qk_ref, zb_ref, zf_ref, hfull_ref, *, n_x_tiles, n_tiles):
    mod = mod_ref[...]
    g = g_ref[...]
    nm = lambda x: _norm_mod(x, g, mod[0:1], mod[1:2])
    t = pl.program_id(1)
    has_prev = jnp.logical_and(t != 0, t != n_x_tiles)
    has_next = jnp.logical_and(t != n_x_tiles - 1, t != n_tiles - 1)
    hm = nm(r_ref[...]).astype(BF16)
    hfull_ref[0:HALO, :] = jnp.where(has_prev, nm(rp_ref[...]), 0.0).astype(BF16)
    hfull_ref[HALO:HALO + TOKEN_TILE, :] = hm
    hfull_ref[HALO + TOKEN_TILE:, :] = jnp.where(has_next, nm(rn_ref[...]), 0.0).astype(BF16)
    hfull = hfull_ref[...]
    rows = TOKEN_TILE + 2 * HALO
    width = wqk_ref.shape[1]
    cw = 512
    for ci in range(width // cw):
        cs = slice(ci * cw, (ci + 1) * cw)
        z = _dot(hfull, wqk_ref[:, cs])
        w = wc_ref[:, cs]
        cv = w[0:1] * pltpu.roll(z, 1, 0) + w[1:2] * z + w[2:3] * pltpu.roll(z, rows - 1, 0)
        cv = cv[HALO:HALO + TOKEN_TILE]
        y = cv * jax.nn.sigmoid(cv)
        if ci * cw < width // 2:
            y = y * (M_HEAD_DIM ** -0.5)
        qk_ref[:, cs] = y.astype(BF16)
    zb_ref[...] = _dot(hm, wb_ref[...]).astype(BF16)
    zf_ref[...] = _dot(hm, wf_ref[...]) + bf_ref[...]


def _project(R, mods, g, wqk, wb, wf, bf, wconv, l, n_x_tiles):
    B, T, D = R.shape
    nt = T // TOKEN_TILE
    hb = TOKEN_TILE // HALO
    n_halo_blocks = T // HALO
    who = lambda b, t: jnp.where(t >= n_x_tiles, B, b)
    nqk, nb, nf = wqk.shape[2], wb.shape[2], wf.shape[2]
    return pl.pallas_call(
        functools.partial(_proj_kernel, n_x_tiles=n_x_tiles, n_tiles=nt),
        grid=(B, nt),
        in_specs=[
            pl.BlockSpec((None, TOKEN_TILE, D), lambda b, t: (b, t, 0)),
            pl.BlockSpec((None, HALO, D), lambda b, t: (b, jnp.maximum(t * hb - 1, 0), 0)),
            pl.BlockSpec((None, HALO, D), lambda b, t: (b, jnp.minimum((t + 1) * hb, n_halo_blocks - 1), 0)),
            pl.BlockSpec((None, None, None, 3, D), lambda b, t: (l, who(b, t), 1, 0, 0)),
            pl.BlockSpec((None, 1, D), lambda b, t: (l, 0, 0)),
            _resident((None, D, nqk), lambda b, t: (l, 0, 0)),
            _resident((None, D, nb), lambda b, t: (l, 0, 0)),
            _resident((None, D, nf), lambda b, t: (l, 0, 0)),
            pl.BlockSpec((None, 1, nf), lambda b, t: (l, 0, 0)),
            pl.BlockSpec((None, M_CONV, nqk), lambda b, t: (l, 0, 0)),
        ],
        out_specs=[
            pl.BlockSpec((None, TOKEN_TILE, nqk), lambda b, t: (b, t, 0)),
            pl.BlockSpec((None, TOKEN_TILE, nb), lambda b, t: (b, t, 0)),
            pl.BlockSpec((None, TOKEN_TILE, nf), lambda b, t: (b, t, 0)),
        ],
        out_shape=[
            jax.ShapeDtypeStruct((B, T, nqk), BF16),
            jax.ShapeDtypeStruct((B, T, nb), BF16),
            jax.ShapeDtypeStruct((B, T, nf), F32),
        ],
        scratch_shapes=[pltpu.VMEM((TOKEN_TILE + 2 * HALO, D), BF16)],
        compiler_params=_params(2),
        name="in_projection",
    )(R, R, R, mods, g, wqk, wb, wf, bf, wconv)


def _mlstm_kernel(q_ref, k_ref, v_ref, gc_ref, gr_ref, o_ref, c_ref, *, n_x, seq):
    L = TOKEN_TILE
    dh = q_ref.shape[1]
    row = lax.broadcasted_iota(jnp.int32, (L, L), 0)
    col = lax.broadcasted_iota(jnp.int32, (L, L), 1)
    lower = col <= row
    upper = col >= row

    def step(d, c0, n, m, accumulate):
        tri, tri_t = (lower, upper) if d == 0 else (upper, lower)
        sl = pl.ds(c0, L)
        q, k, v = q_ref[sl, :], k_ref[sl, :], v_ref[sl, :]
        gr = gr_ref[:, sl]
        gc = gc_ref[sl, :]
        ig_row, lf_row = gr[2 * d:2 * d + 1], jax.nn.log_sigmoid(gr[2 * d + 1:2 * d + 2])
        ig_col, lf_col = gc[:, 2 * d:2 * d + 1], jax.nn.log_sigmoid(gc[:, 2 * d + 1:2 * d + 2])
        b_col = jnp.sum(jnp.where(tri, lf_row, 0.0), axis=1, keepdims=True)
        b_row = jnp.sum(jnp.where(tri_t, lf_col, 0.0), axis=0, keepdims=True)
        b_end = jnp.sum(lf_row, axis=1, keepdims=True)
        g_row = ig_row - b_row
        m_new = jnp.maximum(b_end + m, b_end + jnp.max(g_row, axis=1, keepdims=True))
        a_state = jnp.exp(b_end + m - m_new)
        w_end = jnp.exp(b_end - b_col + ig_col - m_new)
        c_old = c_ref[d]
        inter = b_col + m
        mj = jnp.maximum(inter, b_col + jnp.max(jnp.where(tri, g_row, -jnp.inf), axis=1, keepdims=True))
        decay = jnp.exp(jnp.where(tri, b_col + g_row - mj, -jnp.inf))
        s = lax.dot_general(q, k, (((1,), (1,)), ((), ())), preferred_element_type=F32) * decay
        e_inter = jnp.exp(inter - mj)
        num = _dot(s.astype(BF16), v) + e_inter * _dot(q, c_old.astype(BF16))
        den = jnp.sum(s, axis=1, keepdims=True) + e_inter * jnp.sum(q.astype(F32) * n, axis=1, keepdims=True)
        hval = num * (1.0 / jnp.maximum(jnp.abs(den), jnp.exp(-mj)))
        if accumulate:
            o_ref[sl, :] += hval
        else:
            o_ref[sl, :] = hval
        kw = k.astype(F32) * w_end
        c_ref[d] = a_state * c_old + lax.dot_general(kw.astype(BF16), v, (((0,), (0,)), ((), ())),
                                                     preferred_element_type=F32)
        return a_state * n + jnp.sum(kw, axis=0, keepdims=True), m_new

    c_ref[...] = jnp.zeros(c_ref.shape, F32)
    n0 = jnp.zeros((1, dh), F32)
    m0 = jnp.zeros((1, 1), F32)
    nf, mf = step(0, seq, n0, m0, False)
    nb, mb = step(1, seq, n0, m0, True)

    def both(accumulate):
        def body(i, carry):
            nf, mf, nb, mb = carry
            nf, mf = step(0, pl.multiple_of(i * L, L), nf, mf, accumulate)
            nb, mb = step(1, pl.multiple_of((n_x - 1 - i) * L, L), nb, mb, accumulate)
            return nf, mf, nb, mb
        return body

    carry = lax.fori_loop(0, n_x // 2, both(False), (nf, mf, nb, mb))
    lax.fori_loop(n_x // 2, n_x, both(True), carry)


def _mlstm(QK, ZB, gcol, grow, seq):
    B, T, _ = QK.shape
    dh = M_HEAD_DIM
    n_x = seq // TOKEN_TILE
    return pl.pallas_call(
        functools.partial(_mlstm_kernel, n_x=n_x, seq=seq),
        grid=(B, M_HEADS),
        in_specs=[
            pl.BlockSpec((None, T, dh), lambda b, h: (b, 0, h)),
            pl.BlockSpec((None, T, dh), lambda b, h: (b, 0, M_HEADS + h)),
            pl.BlockSpec((None, T, dh), lambda b, h: (b, 0, h)),
            pl.BlockSpec((None, None, T, 4), lambda b, h: (b, h, 0, 0)),
            pl.BlockSpec((None, None, 4, T), lambda b, h: (b, h, 0, 0)),
        ],
        out_specs=pl.BlockSpec((None, T, dh), lambda b, h: (b, 0, h)),
        out_shape=jax.ShapeDtypeStruct((B, T, M_HEADS * dh), F32),
        scratch_shapes=[pltpu.VMEM((2, dh, dh), F32)],
        compiler_params=_params(2),
        name="mlstm_scan",
    )(QK, QK, ZB, gcol, grow)


def _mla_prep_kernel(zf_ref, cos_ref, sa_ref, sb_ref, gq_ref, gkv_ref, wuq_ref, wuk_ref, wuv_ref,
                     aq_ref, akt_ref, av_ref, *, scale):
    zf = zf_ref[...]
    cqn = _rms(zf[:, :A_QRANK], gq_ref[...]).astype(BF16)
    ckvn = _rms(zf[:, A_QRANK:ZF_KR], gkv_ref[...]).astype(BF16)
    cos, sa, sb = cos_ref[...], sa_ref[...], sb_ref[...]

    def rope(x):
        return x * cos + pltpu.roll(x, A_ROPE // 2, 1) * sa + pltpu.roll(x, LANE - A_ROPE // 2, 1) * sb

    qf = _dot(cqn, wuq_ref[...])
    for h in range(A_HEADS):
        o = h * A_QK
        aq_ref[:, o:o + A_NOPE] = (qf[:, o:o + A_NOPE] * scale).astype(BF16)
        aq_ref[:, o + A_NOPE:o + A_QK] = (rope(qf[:, o + A_NOPE:o + A_QK]) * scale).astype(BF16)
    kn = _dot(ckvn, wuk_ref[...])
    kr = rope(zf[:, ZF_KR:ZF_KR + LANE])
    for h in range(A_HEADS):
        kh = jnp.concatenate([kn[:, h * A_NOPE:(h + 1) * A_NOPE], kr], axis=1)
        akt_ref[h * A_QK:(h + 1) * A_QK, :] = kh.T.astype(BF16)
    av_ref[...] = _dot(ckvn, wuv_ref[...]).astype(BF16)


def _mla_prep(ZF, cos_t, sin_a, sin_b, gq, gkv, wuq, wuk, wuv, l):
    B, T, _ = ZF.shape
    nt = T // TOKEN_TILE
    scale = (A_NOPE + A_ROPE) ** -0.5 * math.log2(math.e)
    return pl.pallas_call(
        functools.partial(_mla_prep_kernel, scale=scale),
        grid=(B, nt),
        in_specs=[
            pl.BlockSpec((None, TOKEN_TILE, ZF_WIDTH), lambda b, t: (b, t, 0)),
            pl.BlockSpec((TOKEN_TILE, LANE), lambda b, t: (t, 0)),
            pl.BlockSpec((TOKEN_TILE, LANE), lambda b, t: (t, 0)),
            pl.BlockSpec((TOKEN_TILE, LANE), lambda b, t: (t, 0)),
            pl.BlockSpec((None, 1, A_QRANK), lambda b, t: (l, 0, 0)),
            pl.BlockSpec((None, 1, A_KVRANK), lambda b, t: (l, 0, 0)),
            _resident((None, A_QRANK, A_HEADS * A_QK), lambda b, t: (l, 0, 0)),
            _resident((None, A_KVRANK, A_HEADS * A_NOPE), lambda b, t: (l, 0, 0)),
            _resident((None, A_KVRANK, A_HEADS * A_VDIM), lambda b, t: (l, 0, 0)),
        ],
        out_specs=[
            pl.BlockSpec((None, TOKEN_TILE, A_HEADS * A_QK), lambda b, t: (b, t, 0)),
            pl.BlockSpec((None, A_HEADS * A_QK, TOKEN_TILE), lambda b, t: (b, 0, t)),
            pl.BlockSpec((None, TOKEN_TILE, A_HEADS * A_VDIM), lambda b, t: (b, t, 0)),
        ],
        out_shape=[
            jax.ShapeDtypeStruct((B, T, A_HEADS * A_QK), BF16),
            jax.ShapeDtypeStruct((B, A_HEADS * A_QK, T), BF16),
            jax.ShapeDtypeStruct((B, T, A_HEADS * A_VDIM), BF16),
        ],
        compiler_params=_params(2),
        name="mla_up_projection",
    )(ZF, cos_t, sin_a, sin_b, gq, gkv, wuq, wuk, wuv)


def _attn_kernel(q_ref, kt_ref, v_ref, *rest, k_lo, k_hi):
    o_ref, vaug_ref = rest[-2], rest[-1]
    nk = k_hi - k_lo

    @pl.when(pl.program_id(2) == 0)
    def _():
        vaug_ref[:, :A_VDIM] = v_ref[k_lo:k_hi, :]
        lane = lax.broadcasted_iota(jnp.int32, (nk, LANE), 1)
        vaug_ref[:, A_VDIM:] = jnp.where(lane == 0, 1.0, 0.0).astype(BF16)

    s = _dot(q_ref[...], kt_ref[:, k_lo:k_hi])
    p = jnp.exp2(s - jnp.max(s, axis=1, keepdims=True)).astype(BF16)
    acc = _dot(p, vaug_ref[...])
    o_ref[...] = (acc[:, :A_VDIM] * (1.0 / acc[:, A_VDIM:A_VDIM + 1])).astype(o_ref.dtype)


def _attention(AQ, AKT, AV, seq):
    B, T, _ = AQ.shape
    ctx_len = T - seq
    tq = ATTN_Q_TILE
    kv_specs = [
        pl.BlockSpec((None, A_QK, T), lambda b, h, t: (b, h, 0)),
        pl.BlockSpec((None, T, A_VDIM), lambda b, h, t: (b, 0, h)),
    ]
    out_shape = jax.ShapeDtypeStruct((B, T, A_HEADS * A_VDIM), BF16)
    A = pl.pallas_call(
        functools.partial(_attn_kernel, k_lo=0, k_hi=T),
        grid=(B, A_HEADS, seq // tq),
        in_specs=[pl.BlockSpec((None, tq, A_QK), lambda b, h, t: (b, t, h))] + kv_specs,
        out_specs=pl.BlockSpec((None, tq, A_VDIM), lambda b, h, t: (b, t, h)),
        out_shape=out_shape,
        scratch_shapes=[pltpu.VMEM((T, A_VDIM + LANE), BF16)],
        compiler_params=_params(3),
        name="mla_attention_latent",
    )(AQ, AKT, AV)
    ctx_block = seq // ctx_len
    return pl.pallas_call(
        functools.partial(_attn_kernel, k_lo=seq, k_hi=T),
        grid=(B, A_HEADS, 1),
        in_specs=[pl.BlockSpec((None, ctx_len, A_QK), lambda b, h, t: (b, ctx_block, h))] + kv_specs
                 + [pl.BlockSpec(memory_space=pl.ANY)],
        out_specs=pl.BlockSpec((None, ctx_len, A_VDIM), lambda b, h, t: (b, ctx_block, h)),
        out_shape=out_shape,
        input_output_aliases={3: 0},
        scratch_shapes=[pltpu.VMEM((ctx_len, A_VDIM + LANE), BF16)],
        compiler_params=_params(3),
        name="mla_attention_context",
    )(AQ, AKT, AV, A)


def _merge_kernel(h_ref, mo_ref, bg_ref, a_ref, r_ref, mod_ref, gmh_ref, wbm_ref, wba_ref, wout_ref, o_ref):
    hh = h_ref[...]
    dh = M_HEAD_DIM
    hn = jnp.concatenate(
        [hh[:, i * dh:(i + 1) * dh]
         * lax.rsqrt(jnp.mean(hh[:, i * dh:(i + 1) * dh] * hh[:, i * dh:(i + 1) * dh], axis=-1, keepdims=True) + EPS)
         for i in range(M_HEADS)], axis=1) * gmh_ref[...]
    hm = (jax.nn.sigmoid(mo_ref[...].astype(F32)) * hn).astype(BF16)
    width = hh.shape[1]
    bg = bg_ref[...].astype(F32)
    u = jax.nn.sigmoid(bg[:, :width]) * _dot(hm, wbm_ref[...]) + jax.nn.sigmoid(bg[:, width:]) * _dot(a_ref[...], wba_ref[...])
    y = _dot(u.astype(BF16), wout_ref[...])
    o_ref[...] = r_ref[...] + mod_ref[...][2:3] * y


def _merge(H, ZB, A, R, mods, gmh, wbm, wba, wout, l, n_x_tiles):
    B, T, D = R.shape
    nt = T // TOKEN_TILE
    W = H.shape[2]
    who = lambda b, t: jnp.where(t >= n_x_tiles, B, b)
    return pl.pallas_call(
        _merge_kernel,
        grid=(B, nt),
        in_specs=[
            pl.BlockSpec((None, TOKEN_TILE, W), lambda b, t: (b, t, 0)),
            pl.BlockSpec((None, TOKEN_TILE, W), lambda b, t: (b, t, 1)),
            pl.BlockSpec((None, TOKEN_TILE, 2 * D), lambda b, t: (b, t, 1)),
            pl.BlockSpec((None, TOKEN_TILE, A.shape[2]), lambda b, t: (b, t, 0)),
            pl.BlockSpec((None, TOKEN_TILE, D), lambda b, t: (b, t, 0)),
            pl.BlockSpec((None, None, None, 3, D), lambda b, t: (l, who(b, t), 1, 0, 0)),
            pl.BlockSpec((None, 1, W), lambda b, t: (l, 0, 0)),
            _resident((None, W, D), lambda b, t: (l, 0, 0)),
            _resident((None, A.shape[2], D), lambda b, t: (l, 0, 0)),
            _resident((None, D, D), lambda b, t: (l, 0, 0)),
        ],
        out_specs=pl.BlockSpec((None, TOKEN_TILE, D), lambda b, t: (b, t, 0)),
        out_shape=jax.ShapeDtypeStruct(R.shape, F32),
        compiler_params=_params(2),
        name="mixer_merge",
    )(H, ZB, ZB, A, R, mods, gmh, wbm, wba, wout)


def _final_kernel(r_ref, g_ref, o_ref):
    o_ref[...] = _rms(r_ref[...], g_ref[...])


def _final_norm(R, g, seq):
    B, T, D = R.shape
    return pl.pallas_call(
        _final_kernel,
        grid=(B, seq // TOKEN_TILE),
        in_specs=[
            pl.BlockSpec((None, TOKEN_TILE, D), lambda b, t: (b, t, 0)),
            pl.BlockSpec((1, D), lambda b, t: (0, 0)),
        ],
        out_specs=pl.BlockSpec((None, TOKEN_TILE, D), lambda b, t: (b, t, 0)),
        out_shape=jax.ShapeDtypeStruct((B, seq, D), F32),
        compiler_params=_params(2),
        name="final_norm",
    )(R, g.reshape(1, D))


def _rope_tables(seq, ctx_len):
    half = A_ROPE // 2
    axis_rot = A_ROPE // 2
    pos = jnp.arange(seq)
    rowp = (pos // GRID_W).astype(F32)
    colp = (pos % GRID_W).astype(F32)
    inv = ROPE_THETA ** (-jnp.arange(0, axis_rot, 2, dtype=F32) / axis_rot)
    ang = jnp.concatenate([rowp[:, None] * inv, colp[:, None] * inv], axis=-1)
    cos, sin = jnp.cos(ang), jnp.sin(ang)
    zeros = jnp.zeros((seq, half), F32)
    pad = jnp.zeros((seq, LANE - A_ROPE), F32)
    cos_x = jnp.concatenate([cos, cos, pad], axis=1)
    sa_x = jnp.concatenate([zeros, sin, pad], axis=1)
    sb_x = jnp.concatenate([-sin, zeros, pad], axis=1)
    cos_c = jnp.concatenate([jnp.ones((ctx_len, A_ROPE), F32), jnp.zeros((ctx_len, LANE - A_ROPE), F32)], axis=1)
    zc = jnp.zeros((ctx_len, LANE), F32)
    return (jnp.concatenate([cos_x, cos_c], axis=0), jnp.concatenate([sa_x, zc], axis=0),
            jnp.concatenate([sb_x, zc], axis=0))


def kernel(x, c, ctx, c_ctx, w_ada, b_ada, g_n1, g_n2, g_n3, w_ff1_up, w_ff1_dn, w_ff2_up, w_ff2_dn, w_in, b_gate, w_conv, g_mh, g_qa, g_kva, w_uq, w_ukv, w_bm, w_ba, w_out, g_final):
    B, S, D = x.shape
    C = ctx.shape[1]
    L = w_ada.shape[0]
    T = S + C
    MW = M_HEADS * M_HEAD_DIM
    assert C == TOKEN_TILE and S % (2 * TOKEN_TILE) == 0 and S % ATTN_Q_TILE == 0 and S % GRID_W == 0
    assert B + 1 <= 8 and D == MW
    n_x_tiles = S // TOKEN_TILE

    perm = jnp.concatenate([jnp.arange(0, A_ROPE, 2), jnp.arange(1, A_ROPE, 2)])
    o_gate = 4 * MW
    o_cq = o_gate + 4 * M_HEADS
    o_ckv = o_cq + A_QRANK
    o_kr = o_ckv + A_KVRANK
    o_br = o_kr + A_ROPE
    wqk = w_in[:, :, :2 * MW].astype(BF16)
    wb = jnp.concatenate([w_in[:, :, 2 * MW:4 * MW], w_in[:, :, o_br:]], axis=2).astype(BF16)
    wf = jnp.concatenate([
        w_in[:, :, o_cq:o_kr], w_in[:, :, o_kr:o_br][:, :, perm], jnp.zeros((L, D, LANE - A_ROPE), F32),
        w_in[:, :, o_gate:o_cq], jnp.zeros((L, D, LANE - 4 * M_HEADS), F32)], axis=2).astype(BF16)
    bf = jnp.concatenate([jnp.zeros((L, ZF_GATE), F32), b_gate, jnp.zeros((L, LANE - 4 * M_HEADS), F32)],
                         axis=1).reshape(L, 1, ZF_WIDTH)
    dq = A_NOPE + A_ROPE
    wuq4 = w_uq.reshape(L, A_QRANK, A_HEADS, dq)
    wuq = jnp.concatenate([wuq4[..., :A_NOPE], wuq4[..., A_NOPE:][..., perm],
                           jnp.zeros((L, A_QRANK, A_HEADS, A_QK - dq), F32)], axis=-1)
    wuq = wuq.reshape(L, A_QRANK, A_HEADS * A_QK).astype(BF16)
    wukv4 = w_ukv.reshape(L, A_KVRANK, A_HEADS, A_NOPE + A_VDIM)
    wuk = wukv4[..., :A_NOPE].reshape(L, A_KVRANK, A_HEADS * A_NOPE).astype(BF16)
    wuv = wukv4[..., A_NOPE:].reshape(L, A_KVRANK, A_HEADS * A_VDIM).astype(BF16)
    wup1, wdn1 = w_ff1_up.astype(BF16), w_ff1_dn.astype(BF16)
    wup2, wdn2 = w_ff2_up.astype(BF16), w_ff2_dn.astype(BF16)
    wbm, wba, wout = w_bm.astype(BF16), w_ba.astype(BF16), w_out.astype(BF16)
    row3 = lambda a: a.reshape(L, 1, a.shape[1])
    gn1, gn2, gn3, gmh, gqa, gkva = map(row3, (g_n1, g_n2, g_n3, g_mh, g_qa, g_kva))
    cos_t, sin_a, sin_b = _rope_tables(S, C)

    cvec = jnp.concatenate([c, c_ctx[None, :], jnp.zeros((8 - B - 1, D), F32)], axis=0)
    mods = _modulations(cvec, w_ada, b_ada).reshape(L, 8, N_MOD // 3, 3, D)

    R = jnp.concatenate([x, ctx], axis=1)
    for l in range(L):
        R = _ffn(R, mods, gn1, wup1, wdn1, l, 0, n_x_tiles)
        QK, ZB, ZF = _project(R, mods, gn2, wqk, wb, wf, bf, w_conv, l, n_x_tiles)
        gates = ZF[:, :, ZF_GATE:ZF_GATE + 4 * M_HEADS].reshape(B, T, 4, M_HEADS)
        gcol = jnp.transpose(gates, (0, 3, 1, 2))
        grow = jnp.transpose(gates, (0, 3, 2, 1))
        H = _mlstm(QK, ZB, gcol, grow, S)
        AQ, AKT, AV = _mla_prep(ZF, cos_t, sin_a, sin_b, gqa, gkva, wuq, wuk, wuv, l)
        A = _attention(AQ, AKT, AV, S)
        R = _merge(H, ZB, A, R, mods, gmh, wbm, wba, wout, l, n_x_tiles)
        R = _ffn(R, mods, gn3, wup2, wdn2, l, 2, n_x_tiles)
    return _final_norm(R, g_final, S)
```

```python
import functools
import math

import jax
import jax.numpy as jnp
from jax import lax
from jax.experimental import pallas as pl
from jax.experimental.pallas import tpu as pltpu

F32 = jnp.float32
BF16 = jnp.bfloat16
EPS = 1e-6

GRID_W = 64
ROPE_THETA = 10000.0
M_HEADS = 4
M_HEAD_DIM = 256
M_CONV = 3
A_HEADS = 8
A_NOPE = 128
A_ROPE = 64
A_VDIM = 128
A_QRANK = 384
A_KVRANK = 256
N_MOD = 9

TOKEN_TILE = 256
HALO = 16
ATTN_Q_TILE = 1024
LANE = 128
A_QK = 2 * LANE
ZF_WIDTH = A_QRANK + A_KVRANK + LANE + LANE
ZF_KR = A_QRANK + A_KVRANK
ZF_GATE = ZF_KR + LANE
VMEM_LIMIT = 48 * 1024 * 1024


def _dot(a, b):
    return jnp.dot(a, b, preferred_element_type=F32)


def _rms(x, g):
    return x * lax.rsqrt(jnp.mean(x * x, axis=-1, keepdims=True) + EPS) * g


def _norm_mod(x, g, shift, scale):
    return _rms(x, g) * (1 + scale) + shift


def _params(n_axes, vmem=VMEM_LIMIT):
    return pltpu.CompilerParams(dimension_semantics=("arbitrary",) * n_axes, vmem_limit_bytes=vmem)


def _resident(block, index_map):
    return pl.BlockSpec(block, index_map, pipeline_mode=pl.Buffered(1))


def _mod_kernel(c_ref, w_ref, b_ref, o_ref):
    cv = c_ref[...]
    s = (cv * jax.nn.sigmoid(cv)).astype(BF16)
    o_ref[...] = _dot(s, w_ref[...].astype(BF16)) + b_ref[...]


def _modulations(cvec, w_ada, b_ada):
    L, D, N = w_ada.shape
    tn = 1024
    return pl.pallas_call(
        _mod_kernel,
        grid=(L, N // tn),
        in_specs=[
            pl.BlockSpec((8, D), lambda l, j: (0, 0)),
            pl.BlockSpec((None, D, tn), lambda l, j: (l, 0, j)),
            pl.BlockSpec((None, 1, tn), lambda l, j: (l, 0, j)),
        ],
        out_specs=pl.BlockSpec((None, 8, tn), lambda l, j: (l, 0, j)),
        out_shape=jax.ShapeDtypeStruct((L, 8, N), F32),
        compiler_params=_params(2),
        name="adaln_mod",
    )(cvec, w_ada, b_ada.reshape(L, 1, N))


def _ffn_kernel(r_ref, mod_ref, g_ref, wup_ref, wdn_ref, o_ref, *, d_ff, n_chunks):
    x = r_ref[...]
    mod = mod_ref[...]
    h = _norm_mod(x, g_ref[...], mod[0:1], mod[1:2]).astype(BF16)
    tf = d_ff // n_chunks
    acc = None
    for ci in range(n_chunks):
        a = _dot(h, wup_ref[:, ci * tf:(ci + 1) * tf])
        b = _dot(h, wup_ref[:, d_ff + ci * tf:d_ff + (ci + 1) * tf])
        act = (a * jax.nn.sigmoid(a) * b).astype(BF16)
        part = _dot(act, wdn_ref[ci * tf:(ci + 1) * tf, :])
        acc = part if acc is None else acc + part
    o_ref[...] = x + (0.5 * mod[2:3]) * acc


def _ffn(R, mods, g, w_up, w_dn, l, sub, n_x_tiles):
    B, T, D = R.shape
    d_ff = w_dn.shape[1]
    nt = T // TOKEN_TILE
    who = lambda b, t: jnp.where(t >= n_x_tiles, B, b)
    return pl.pallas_call(
        functools.partial(_ffn_kernel, d_ff=d_ff, n_chunks=2),
        grid=(B, nt),
        in_specs=[
            pl.BlockSpec((None, TOKEN_TILE, D), lambda b, t: (b, t, 0)),
            pl.BlockSpec((None, None, None, 3, D), lambda b, t: (l, who(b, t), sub, 0, 0)),
            pl.BlockSpec((None, 1, D), lambda b, t: (l, 0, 0)),
            _resident((None, D, 2 * d_ff), lambda b, t: (l, 0, 0)),
            _resident((None, d_ff, D), lambda b, t: (l, 0, 0)),
        ],
        out_specs=pl.BlockSpec((None, TOKEN_TILE, D), lambda b, t: (b, t, 0)),
        out_shape=jax.ShapeDtypeStruct(R.shape, F32),
        compiler_params=_params(2),
        name="ffn_half_step",
    )(R, mods, g, w_up, w_dn)


def _proj_kernel(r_ref, rp_ref, rn_ref, mod_ref, g_ref, wqk_ref, wb_ref, wf_ref, bf_ref, wc_ref,
                 qk_ref, zb_ref, zf_ref, hfull_ref, *, n_x_tiles, n_tiles):
    mod = mod_ref[...]
    g = g_ref[...]
    nm = lambda x: _norm_mod(x, g, mod[0:1], mod[1:2])
    t = pl.program_id(1)
    has_prev = jnp.logical_and(t != 0, t != n_x_tiles)
    has_next = jnp.logical_and(t != n_x_tiles - 1, t != n_tiles - 1)
    hm = nm(r_ref[...]).astype(BF16)
    hfull_ref[0:HALO, :] = jnp.where(has_prev, nm(rp_ref[...]), 0.0).astype(BF16)
    hfull_ref[HALO:HALO + TOKEN_TILE, :] = hm
    hfull_ref[HALO + TOKEN_TILE:, :] = jnp.where(has_next, nm(rn_ref[...]), 0.0).astype(BF16)
    hfull = hfull_ref[...]
    rows = TOKEN_TILE + 2 * HALO
    width = wqk_ref.shape[1]
    cw = 512
    for ci in range(width // cw):
        cs = slice(ci * cw, (ci + 1) * cw)
        z = _dot(hfull, wqk_ref[:, cs])
        w = wc_ref[:, cs]
        cv = w[0:1] * pltpu.roll(z, 1, 0) + w[1:2] * z + w[2:3] * pltpu.roll(z, rows - 1, 0)
        cv = cv[HALO:HALO + TOKEN_TILE]
        y = cv * jax.nn.sigmoid(cv)
        if ci * cw < width // 2:
            y = y * (M_HEAD_DIM ** -0.5)
        qk_ref[:, cs] = y.astype(BF16)
    zb_ref[...] = _dot(hm, wb_ref[...]).astype(BF16)
    zf_ref[...] = _dot(hm, wf_ref[...]) + bf_ref[...]


def _project(R, mods, g, wqk, wb, wf, bf, wconv, l, n_x_tiles):
    B, T, D = R.shape
    nt = T // TOKEN_TILE
    hb = TOKEN_TILE // HALO
    n_halo_blocks = T // HALO
    who = lambda b, t: jnp.where(t >= n_x_tiles, B, b)
    nqk, nb, nf = wqk.shape[2], wb.shape[2], wf.shape[2]
    return pl.pallas_call(
        functools.partial(_proj_kernel, n_x_tiles=n_x_tiles, n_tiles=nt),
        grid=(B, nt),
        in_specs=[
            pl.BlockSpec((None, TOKEN_TILE, D), lambda b, t: (b, t, 0)),
            pl.BlockSpec((None, HALO, D), lambda b, t: (b, jnp.maximum(t * hb - 1, 0), 0)),
            pl.BlockSpec((None, HALO, D), lambda b, t: (b, jnp.minimum((t + 1) * hb, n_halo_blocks - 1), 0)),
            pl.BlockSpec((None, None, None, 3, D), lambda b, t: (l, who(b, t), 1, 0, 0)),
            pl.BlockSpec((None, 1, D), lambda b, t: (l, 0, 0)),
            _resident((None, D, nqk), lambda b, t: (l, 0, 0)),
            _resident((None, D, nb), lambda b, t: (l, 0, 0)),
            _resident((None, D, nf), lambda b, t: (l, 0, 0)),
            pl.BlockSpec((None, 1, nf), lambda b, t: (l, 0, 0)),
            pl.BlockSpec((None, M_CONV, nqk), lambda b, t: (l, 0, 0)),
        ],
        out_specs=[
            pl.BlockSpec((None, TOKEN_TILE, nqk), lambda b, t: (b, t, 0)),
            pl.BlockSpec((None, TOKEN_TILE, nb), lambda b, t: (b, t, 0)),
            pl.BlockSpec((None, TOKEN_TILE, nf), lambda b, t: (b, t, 0)),
        ],
        out_shape=[
            jax.ShapeDtypeStruct((B, T, nqk), BF16),
            jax.ShapeDtypeStruct((B, T, nb), BF16),
            jax.ShapeDtypeStruct((B, T, nf), F32),
        ],
        scratch_shapes=[pltpu.VMEM((TOKEN_TILE + 2 * HALO, D), BF16)],
        compiler_params=_params(2),
        name="in_projection",
    )(R, R, R, mods, g, wqk, wb, wf, bf, wconv)


def _mlstm_kernel(q_ref, k_ref, v_ref, gc_ref, gr_ref, o_ref, c_ref, *, n_x, seq):
    L = TOKEN_TILE
    dh = q_ref.shape[1]
    row = lax.broadcasted_iota(jnp.int32, (L, L), 0)
    col = lax.broadcasted_iota(jnp.int32, (L, L), 1)
    lower = col <= row
    upper = col >= row

    def step(d, c0, n, m, accumulate):
        tri, tri_t = (lower, upper) if d == 0 else (upper, lower)
        sl = pl.ds(c0, L)
        q, k, v = q_ref[sl, :], k_ref[sl, :], v_ref[sl, :]
        gr = gr_ref[:, sl]
        gc = gc_ref[sl, :]
        ig_row, lf_row = gr[2 * d:2 * d + 1], jax.nn.log_sigmoid(gr[2 * d + 1:2 * d + 2])
        ig_col, lf_col = gc[:, 2 * d:2 * d + 1], jax.nn.log_sigmoid(gc[:, 2 * d + 1:2 * d + 2])
        b_col = jnp.sum(jnp.where(tri, lf_row, 0.0), axis=1, keepdims=True)
        b_row = jnp.sum(jnp.where(tri_t, lf_col, 0.0), axis=0, keepdims=True)
        b_end = jnp.sum(lf_row, axis=1, keepdims=True)
        g_row = ig_row - b_row
        m_new = jnp.maximum(b_end + m, b_end + jnp.max(g_row, axis=1, keepdims=True))
        a_state = jnp.exp(b_end + m - m_new)
        w_end = jnp.exp(b_end - b_col + ig_col - m_new)
        c_old = c_ref[d]
        inter = b_col + m
        mj = jnp.maximum(inter, b_col + jnp.max(jnp.where(tri, g_row, -jnp.inf), axis=1, keepdims=True))
        decay = jnp.exp(jnp.where(tri, b_col + g_row - mj, -jnp.inf))
        s = lax.dot_general(q, k, (((1,), (1,)), ((), ())), preferred_element_type=F32) * decay
        e_inter = jnp.exp(inter - mj)
        num = _dot(s.astype(BF16), v) + e_inter * _dot(q, c_old.astype(BF16))
        den = jnp.sum(s, axis=1, keepdims=True) + e_inter * jnp.sum(q.astype(F32) * n, axis=1, keepdims=True)
        hval = num * (1.0 / jnp.maximum(jnp.abs(den), jnp.exp(-mj)))
        if accumulate:
            o_ref[sl, :] += hval
        else:
            o_ref[sl, :] = hval
        kw = k.astype(F32) * w_end
        c_ref[d] = a_state * c_old + lax.dot_general(kw.astype(BF16), v, (((0,), (0,)), ((), ())),
                                                     preferred_element_type=F32)
        return a_state * n + jnp.sum(kw, axis=0, keepdims=True), m_new

    c_ref[...] = jnp.zeros(c_ref.shape, F32)
    n0 = jnp.zeros((1, dh), F32)
    m0 = jnp.zeros((1, 1), F32)
    nf, mf = step(0, seq, n0, m0, False)
    nb, mb = step(1, seq, n0, m0, True)

    def both(accumulate):
        def body(i, carry):
            nf, mf, nb, mb = carry
            nf, mf = step(0, pl.multiple_of(i * L, L), nf, mf, accumulate)
            nb, mb = step(1, pl.multiple_of((n_x - 1 - i) * L, L), nb, mb, accumulate)
            return nf, mf, nb, mb
        return body

    carry = lax.fori_loop(0, n_x // 2, both(False), (nf, mf, nb, mb))
    lax.fori_loop(n_x // 2, n_x, both(True), carry)


def _mlstm(QK, ZB, gcol, grow, seq):
    B, T, _ = QK.shape
    dh = M_HEAD_DIM
    n_x = seq // TOKEN_TILE
    return pl.pallas_call(
        functools.partial(_mlstm_kernel, n_x=n_x, seq=seq),
        grid=(B, M_HEADS),
        in_specs=[
            pl.BlockSpec((None, T, dh), lambda b, h: (b, 0, h)),
            pl.BlockSpec((None, T, dh), lambda b, h: (b, 0, M_HEADS + h)),
            pl.BlockSpec((None, T, dh), lambda b, h: (b, 0, h)),
            pl.BlockSpec((None, None, T, 4), lambda b, h: (b, h, 0, 0)),
            pl.BlockSpec((None, None, 4, T), lambda b, h: (b, h, 0, 0)),
        ],
        out_specs=pl.BlockSpec((None, T, dh), lambda b, h: (b, 0, h)),
        out_shape=jax.ShapeDtypeStruct((B, T, M_HEADS * dh), F32),
        scratch_shapes=[pltpu.VMEM((2, dh, dh), F32)],
        compiler_params=_params(2),
        name="mlstm_scan",
    )(QK, QK, ZB, gcol, grow)


def _mla_prep_kernel(zf_ref, cos_ref, sa_ref, sb_ref, gq_ref, gkv_ref, wuq_ref, wuk_ref, wuv_ref,
                     aq_ref, akt_ref, av_ref, *, scale):
    zf = zf_ref[...]
    cqn = _rms(zf[:, :A_QRANK], gq_ref[...]).astype(BF16)
    ckvn = _rms(zf[:, A_QRANK:ZF_KR], gkv_ref[...]).astype(BF16)
    cos, sa, sb = cos_ref[...], sa_ref[...], sb_ref[...]

    def rope(x):
        return x * cos + pltpu.roll(x, A_ROPE // 2, 1) * sa + pltpu.roll(x, LANE - A_ROPE // 2, 1) * sb

    qf = _dot(cqn, wuq_ref[...])
    for h in range(A_HEADS):
        o = h * A_QK
        aq_ref[:, o:o + A_NOPE] = (qf[:, o:o + A_NOPE] * scale).astype(BF16)
        aq_ref[:, o + A_NOPE:o + A_QK] = (rope(qf[:, o + A_NOPE:o + A_QK]) * scale).astype(BF16)
    kn = _dot(ckvn, wuk_ref[...])
    kr = rope(zf[:, ZF_KR:ZF_KR + LANE])
    for h in range(A_HEADS):
        kh = jnp.concatenate([kn[:, h * A_NOPE:(h + 1) * A_NOPE], kr], axis=1)
        akt_ref[h * A_QK:(h + 1) * A_QK, :] = kh.T.astype(BF16)
    av_ref[...] = _dot(ckvn, wuv_ref[...]).astype(BF16)


def _mla_prep(ZF, cos_t, sin_a, sin_b, gq, gkv, wuq, wuk, wuv, l):
    B, T, _ = ZF.shape
    nt = T // TOKEN_TILE
    scale = (A_NOPE + A_ROPE) ** -0.5 * math.log2(math.e)
    return pl.pallas_call(
        functools.partial(_mla_prep_kernel, scale=scale),
        grid=(B, nt),
        in_specs=[
            pl.BlockSpec((None, TOKEN_TILE, ZF_WIDTH), lambda b, t: (b, t, 0)),
            pl.BlockSpec((TOKEN_TILE, LANE), lambda b, t: (t, 0)),
            pl.BlockSpec((TOKEN_TILE, LANE), lambda b, t: (t, 0)),
            pl.BlockSpec((TOKEN_TILE, LANE), lambda b, t: (t, 0)),
            pl.BlockSpec((None, 1, A_QRANK), lambda b, t: (l, 0, 0)),
            pl.BlockSpec((None, 1, A_KVRANK), lambda b, t: (l, 0, 0)),
            _resident((None, A_QRANK, A_HEADS * A_QK), lambda b, t: (l, 0, 0)),
            _resident((None, A_KVRANK, A_HEADS * A_NOPE), lambda b, t: (l, 0, 0)),
            _resident((None, A_KVRANK, A_HEADS * A_VDIM), lambda b, t: (l, 0, 0)),
        ],
        out_specs=[
            pl.BlockSpec((None, TOKEN_TILE, A_HEADS * A_QK), lambda b, t: (b, t, 0)),
            pl.BlockSpec((None, A_HEADS * A_QK, TOKEN_TILE), lambda b, t: (b, 0, t)),
            pl.BlockSpec((None, TOKEN_TILE, A_HEADS * A_VDIM), lambda b, t: (b, t, 0)),
        ],
        out_shape=[
            jax.ShapeDtypeStruct((B, T, A_HEADS * A_QK), BF16),
            jax.ShapeDtypeStruct((B, A_HEADS * A_QK, T), BF16),
            jax.ShapeDtypeStruct((B, T, A_HEADS * A_VDIM), BF16),
        ],
        compiler_params=_params(2),
        name="mla_up_projection",
    )(ZF, cos_t, sin_a, sin_b, gq, gkv, wuq, wuk, wuv)


def _attn_kernel(q_ref, kt_ref, v_ref, o_ref, vaug_ref, *, n_split):
    nk = v_ref.shape[0]

    @pl.when(pl.program_id(2) == 0)
    def _():
        vaug_ref[:, :A_VDIM] = v_ref[...]
        lane = lax.broadcasted_iota(jnp.int32, (nk, LANE), 1)
        vaug_ref[:, A_VDIM:] = jnp.where(lane == 0, 1.0, 0.0).astype(BF16)

    rows = q_ref.shape[0] // n_split
    scores = [_dot(q_ref[r * rows:(r + 1) * rows, :], kt_ref[...]) for r in range(n_split)]
    for r, s in enumerate(scores):
        rs = slice(r * rows, (r + 1) * rows)
        p = jnp.exp2(s - jnp.max(s, axis=1, keepdims=True)).astype(BF16)
        acc = _dot(p, vaug_ref[...])
        o_ref[rs, :] = (acc[:, :A_VDIM] * (1.0 / acc[:, A_VDIM:A_VDIM + 1])).astype(o_ref.dtype)


def _attention(AQ, AKT, AV, seq):
    B, T, _ = AQ.shape
    ctx_len = T - seq
    tq = ATTN_Q_TILE
    width = A_HEADS * A_VDIM
    a_lat = pl.pallas_call(
        functools.partial(_attn_kernel, n_split=2),
        grid=(B, A_HEADS, seq // tq),
        in_specs=[
            pl.BlockSpec((None, tq, A_QK), lambda b, h, t: (b, t, h)),
            pl.BlockSpec((None, A_QK, T), lambda b, h, t: (b, h, 0)),
            pl.BlockSpec((None, T, A_VDIM), lambda b, h, t: (b, 0, h)),
        ],
        out_specs=pl.BlockSpec((None, tq, A_VDIM), lambda b, h, t: (b, t, h)),
        out_shape=jax.ShapeDtypeStruct((B, seq, width), BF16),
        scratch_shapes=[pltpu.VMEM((T, A_VDIM + LANE), BF16)],
        compiler_params=_params(3),
        name="mla_attention_latent",
    )(AQ, AKT, AV)
    cb = seq // ctx_len
    a_ctx = pl.pallas_call(
        functools.partial(_attn_kernel, n_split=1),
        grid=(B, A_HEADS, 1),
        in_specs=[
            pl.BlockSpec((None, ctx_len, A_QK), lambda b, h, t: (b, cb, h)),
            pl.BlockSpec((None, A_QK, ctx_len), lambda b, h, t: (b, h, cb)),
            pl.BlockSpec((None, ctx_len, A_VDIM), lambda b, h, t: (b, cb, h)),
        ],
        out_specs=pl.BlockSpec((None, ctx_len, A_VDIM), lambda b, h, t: (b, 0, h)),
        out_shape=jax.ShapeDtypeStruct((B, ctx_len, width), BF16),
        scratch_shapes=[pltpu.VMEM((ctx_len, A_VDIM + LANE), BF16)],
        compiler_params=_params(3),
        name="mla_attention_context",
    )(AQ, AKT, AV)
    return a_lat, a_ctx


def _merge_kernel(h_ref, mo_ref, bg_ref, al_ref, ac_ref, r_ref, mod_ref, gmh_ref, wbm_ref, wba_ref, wout_ref, o_ref,
                  *, n_x_tiles):
    a = jnp.where(pl.program_id(1) < n_x_tiles, al_ref[...], ac_ref[...])
    hh = h_ref[...]
    dh = M_HEAD_DIM
    hn = jnp.concatenate(
        [hh[:, i * dh:(i + 1) * dh]
         * lax.rsqrt(jnp.mean(hh[:, i * dh:(i + 1) * dh] * hh[:, i * dh:(i + 1) * dh], axis=-1, keepdims=True) + EPS)
         for i in range(M_HEADS)], axis=1) * gmh_ref[...]
    hm = (jax.nn.sigmoid(mo_ref[...].astype(F32)) * hn).astype(BF16)
    width = hh.shape[1]
    bg = bg_ref[...].astype(F32)
    u = jax.nn.sigmoid(bg[:, :width]) * _dot(hm, wbm_ref[...]) + jax.nn.sigmoid(bg[:, width:]) * _dot(a, wba_ref[...])
    y = _dot(u.astype(BF16), wout_ref[...])
    o_ref[...] = r_ref[...] + mod_ref[...][2:3] * y


def _merge(H, ZB, A_lat, A_ctx, R, mods, gmh, wbm, wba, wout, l, n_x_tiles):
    B, T, D = R.shape
    nt = T // TOKEN_TILE
    W = H.shape[2]
    AW = A_lat.shape[2]
    who = lambda b, t: jnp.where(t >= n_x_tiles, B, b)
    return pl.pallas_call(
        functools.partial(_merge_kernel, n_x_tiles=n_x_tiles),
        grid=(B, nt),
        in_specs=[
            pl.BlockSpec((None, TOKEN_TILE, W), lambda b, t: (b, t, 0)),
            pl.BlockSpec((None, TOKEN_TILE, W), lambda b, t: (b, t, 1)),
            pl.BlockSpec((None, TOKEN_TILE, 2 * D), lambda b, t: (b, t, 1)),
            pl.BlockSpec((None, TOKEN_TILE, AW), lambda b, t: (b, jnp.minimum(t, n_x_tiles - 1), 0)),
            pl.BlockSpec((None, TOKEN_TILE, AW), lambda b, t: (b, 0, 0)),
            pl.BlockSpec((None, TOKEN_TILE, D), lambda b, t: (b, t, 0)),
            pl.BlockSpec((None, None, None, 3, D), lambda b, t: (l, who(b, t), 1, 0, 0)),
            pl.BlockSpec((None, 1, W), lambda b, t: (l, 0, 0)),
            _resident((None, W, D), lambda b, t: (l, 0, 0)),
            _resident((None, AW, D), lambda b, t: (l, 0, 0)),
            _resident((None, D, D), lambda b, t: (l, 0, 0)),
        ],
        out_specs=pl.BlockSpec((None, TOKEN_TILE, D), lambda b, t: (b, t, 0)),
        out_shape=jax.ShapeDtypeStruct(R.shape, F32),
        compiler_params=_params(2),
        name="mixer_merge",
    )(H, ZB, ZB, A_lat, A_ctx, R, mods, gmh, wbm, wba, wout)


def _final_kernel(r_ref, g_ref, o_ref):
    o_ref[...] = _rms(r_ref[...], g_ref[...])


def _final_norm(R, g, seq):
    B, T, D = R.shape
    return pl.pallas_call(
        _final_kernel,
        grid=(B, seq // TOKEN_TILE),
        in_specs=[
            pl.BlockSpec((None, TOKEN_TILE, D), lambda b, t: (b, t, 0)),
            pl.BlockSpec((1, D), lambda b, t: (0, 0)),
        ],
        out_specs=pl.BlockSpec((None, TOKEN_TILE, D), lambda b, t: (b, t, 0)),
        out_shape=jax.ShapeDtypeStruct((B, seq, D), F32),
        compiler_params=_params(2),
        name="final_norm",
    )(R, g.reshape(1, D))


def _rope_tables(seq, ctx_len):
    half = A_ROPE // 2
    axis_rot = A_ROPE // 2
    pos = jnp.arange(seq)
    rowp = (pos // GRID_W).astype(F32)
    colp = (pos % GRID_W).astype(F32)
    inv = ROPE_THETA ** (-jnp.arange(0, axis_rot, 2, dtype=F32) / axis_rot)
    ang = jnp.concatenate([rowp[:, None] * inv, colp[:, None] * inv], axis=-1)
    cos, sin = jnp.cos(ang), jnp.sin(ang)
    zeros = jnp.zeros((seq, half), F32)
    pad = jnp.zeros((seq, LANE - A_ROPE), F32)
    cos_x = jnp.concatenate([cos, cos, pad], axis=1)
    sa_x = jnp.concatenate([zeros, sin, pad], axis=1)
    sb_x = jnp.concatenate([-sin, zeros, pad], axis=1)
    cos_c = jnp.concatenate([jnp.ones((ctx_len, A_ROPE), F32), jnp.zeros((ctx_len, LANE - A_ROPE), F32)], axis=1)
    zc = jnp.zeros((ctx_len, LANE), F32)
    return (jnp.concatenate([cos_x, cos_c], axis=0), jnp.concatenate([sa_x, zc], axis=0),
            jnp.concatenate([sb_x, zc], axis=0))


def kernel(x, c, ctx, c_ctx, w_ada, b_ada, g_n1, g_n2, g_n3, w_ff1_up, w_ff1_dn, w_ff2_up, w_ff2_dn, w_in, b_gate, w_conv, g_mh, g_qa, g_kva, w_uq, w_ukv, w_bm, w_ba, w_out, g_final):
    B, S, D = x.shape
    C = ctx.shape[1]
    L = w_ada.shape[0]
    T = S + C
    MW = M_HEADS * M_HEAD_DIM
    assert C == TOKEN_TILE and S % (2 * TOKEN_TILE) == 0 and S % ATTN_Q_TILE == 0 and S % GRID_W == 0
    assert B + 1 <= 8 and D == MW
    n_x_tiles = S // TOKEN_TILE

    perm = jnp.concatenate([jnp.arange(0, A_ROPE, 2), jnp.arange(1, A_ROPE, 2)])
    o_gate = 4 * MW
    o_cq = o_gate + 4 * M_HEADS
    o_ckv = o_cq + A_QRANK
    o_kr = o_ckv + A_KVRANK
    o_br = o_kr + A_ROPE
    wqk = w_in[:, :, :2 * MW].astype(BF16)
    wb = jnp.concatenate([w_in[:, :, 2 * MW:4 * MW], w_in[:, :, o_br:]], axis=2).astype(BF16)
    wf = jnp.concatenate([
        w_in[:, :, o_cq:o_kr], w_in[:, :, o_kr:o_br][:, :, perm], jnp.zeros((L, D, LANE - A_ROPE), F32),
        w_in[:, :, o_gate:o_cq], jnp.zeros((L, D, LANE - 4 * M_HEADS), F32)], axis=2).astype(BF16)
    bf = jnp.concatenate([jnp.zeros((L, ZF_GATE), F32), b_gate, jnp.zeros((L, LANE - 4 * M_HEADS), F32)],
                         axis=1).reshape(L, 1, ZF_WIDTH)
    dq = A_NOPE + A_ROPE
    wuq4 = w_uq.reshape(L, A_QRANK, A_HEADS, dq)
    wuq = jnp.concatenate([wuq4[..., :A_NOPE], wuq4[..., A_NOPE:][..., perm],
                           jnp.zeros((L, A_QRANK, A_HEADS, A_QK - dq), F32)], axis=-1)
    wuq = wuq.reshape(L, A_QRANK, A_HEADS * A_QK).astype(BF16)
    wukv4 = w_ukv.reshape(L, A_KVRANK, A_HEADS, A_NOPE + A_VDIM)
    wuk = wukv4[..., :A_NOPE].reshape(L, A_KVRANK, A_HEADS * A_NOPE).astype(BF16)
    wuv = wukv4[..., A_NOPE:].reshape(L, A_KVRANK, A_HEADS * A_VDIM).astype(BF16)
    wup1, wdn1 = w_ff1_up.astype(BF16), w_ff1_dn.astype(BF16)
    wup2, wdn2 = w_ff2_up.astype(BF16), w_ff2_dn.astype(BF16)
    wbm, wba, wout = w_bm.astype(BF16), w_ba.astype(BF16), w_out.astype(BF16)
    row3 = lambda a: a.reshape(L, 1, a.shape[1])
    gn1, gn2, gn3, gmh, gqa, gkva = map(row3, (g_n1, g_n2, g_n3, g_mh, g_qa, g_kva))
    cos_t, sin_a, sin_b = _rope_tables(S, C)

    cvec = jnp.concatenate([c, c_ctx[None, :], jnp.zeros((8 - B - 1, D), F32)], axis=0)
    mods = _modulations(cvec, w_ada, b_ada).reshape(L, 8, N_MOD // 3, 3, D)

    R = jnp.concatenate([x, ctx], axis=1)
    for l in range(L):
        R = _ffn(R, mods, gn1, wup1, wdn1, l, 0, n_x_tiles)
        QK, ZB, ZF = _project(R, mods, gn2, wqk, wb, wf, bf, w_conv, l, n_x_tiles)
        gates = ZF[:, :, ZF_GATE:ZF_GATE + 4 * M_HEADS].reshape(B, T, 4, M_HEADS)
        gcol = jnp.transpose(gates, (0, 3, 1, 2))
        grow = jnp.transpose(gates, (0, 3, 2, 1))
        H = _mlstm(QK, ZB, gcol, grow, S)
        AQ, AKT, AV = _mla_prep(ZF, cos_t, sin_a, sin_b, gqa, gkva, wuq, wuk, wuv, l)
        A_lat, A_ctx = _attention(AQ, AKT, AV, S)
        R = _merge(H, ZB, A_lat, A_ctx, R, mods, gmh, wbm, wba, wout, l, n_x_tiles)
        R = _ffn(R, mods, gn3, wup2, wdn2, l, 2, n_x_tiles)
    return _final_norm(R, g_final, S)
```

```python
import functools
import math

import jax
import jax.numpy as jnp
from jax import lax
from jax.experimental import pallas as pl
from jax.experimental.pallas import tpu as pltpu

F32 = jnp.float32
BF16 = jnp.bfloat16
EPS = 1e-6

GRID_W = 64
ROPE_THETA = 10000.0
M_HEADS = 4
M_HEAD_DIM = 256
M_CONV = 3
A_HEADS = 8
A_NOPE = 128
A_ROPE = 64
A_VDIM = 128
A_QRANK = 384
A_KVRANK = 256
N_MOD = 9

TOKEN_TILE = 256
HALO = 16
MLSTM_AUG = 16
MLSTM_UNROLL = 2
ATTN_Q_TILE = 1024
LANE = 128
A_QK = 2 * LANE
ZF_WIDTH = A_QRANK + A_KVRANK + LANE + LANE
ZF_KR = A_QRANK + A_KVRANK
ZF_GATE = ZF_KR + LANE
VMEM_LIMIT = 48 * 1024 * 1024


def _dot(a, b):
    return jnp.dot(a, b, preferred_element_type=F32)


def _rms(x, g):
    return x * lax.rsqrt(jnp.mean(x * x, axis=-1, keepdims=True) + EPS) * g


def _norm_mod(x, g, shift, scale):
    return _rms(x, g) * (1 + scale) + shift


def _params(n_axes, vmem=VMEM_LIMIT):
    return pltpu.CompilerParams(dimension_semantics=("arbitrary",) * n_axes, vmem_limit_bytes=vmem)


def _resident(block, index_map):
    return pl.BlockSpec(block, index_map, pipeline_mode=pl.Buffered(1))


def _mod_kernel(c_ref, w_ref, b_ref, o_ref):
    cv = c_ref[...]
    s = (cv * jax.nn.sigmoid(cv)).astype(BF16)
    o_ref[...] = _dot(s, w_ref[...].astype(BF16)) + b_ref[...]


def _modulations(cvec, w_ada, b_ada):
    L, D, N = w_ada.shape
    tn = 1024
    return pl.pallas_call(
        _mod_kernel,
        grid=(L, N // tn),
        in_specs=[
            pl.BlockSpec((8, D), lambda l, j: (0, 0)),
            pl.BlockSpec((None, D, tn), lambda l, j: (l, 0, j)),
            pl.BlockSpec((None, 1, tn), lambda l, j: (l, 0, j)),
        ],
        out_specs=pl.BlockSpec((None, 8, tn), lambda l, j: (l, 0, j)),
        out_shape=jax.ShapeDtypeStruct((L, 8, N), F32),
        compiler_params=_params(2),
        name="adaln_mod",
    )(cvec, w_ada, b_ada.reshape(L, 1, N))


def _ffn_kernel(*refs, d_ff, n_chunks, n_x_tiles, split_input, final_norm):
    refs = list(refs)
    o_ref = refs.pop()
    gf_ref = refs.pop() if final_norm else None
    if split_input:
        x = jnp.where(pl.program_id(1) < n_x_tiles, refs[0][...], refs[1][...])
        refs = refs[2:]
    else:
        x = refs[0][...]
        refs = refs[1:]
    mod_ref, g_ref, wup_ref, wdn_ref = refs
    mod = mod_ref[...]
    h = _norm_mod(x, g_ref[...], mod[0:1], mod[1:2]).astype(BF16)
    tf = d_ff // n_chunks
    acc = None
    for ci in range(n_chunks):
        a = _dot(h, wup_ref[:, ci * tf:(ci + 1) * tf])
        b = _dot(h, wup_ref[:, d_ff + ci * tf:d_ff + (ci + 1) * tf])
        act = (a * jax.nn.sigmoid(a) * b).astype(BF16)
        part = _dot(act, wdn_ref[ci * tf:(ci + 1) * tf, :])
        acc = part if acc is None else acc + part
    y = x + (0.5 * mod[2:3]) * acc
    o_ref[...] = _rms(y, gf_ref[...]) if final_norm else y


def _ffn(src, mods, g, w_up, w_dn, l, sub, n_x_tiles, final_g=None):
    split_input = isinstance(src, tuple)
    if split_input:
        xs, cs = src
        B, S, D = xs.shape
        T = S + cs.shape[1]
        row_specs = [
            pl.BlockSpec((None, TOKEN_TILE, D), lambda b, t: (b, jnp.minimum(t, n_x_tiles - 1), 0)),
            pl.BlockSpec((None, TOKEN_TILE, D), lambda b, t: (b, 0, 0)),
        ]
        rows = [xs, cs]
    else:
        B, T, D = src.shape
        row_specs = [pl.BlockSpec((None, TOKEN_TILE, D), lambda b, t: (b, t, 0))]
        rows = [src]
    d_ff = w_dn.shape[1]
    final_norm = final_g is not None
    nt = n_x_tiles if final_norm else T // TOKEN_TILE
    who = lambda b, t: jnp.where(t >= n_x_tiles, B, b)
    extra_specs = [pl.BlockSpec((1, D), lambda b, t: (0, 0))] if final_norm else []
    extra = [final_g.reshape(1, D)] if final_norm else []
    return pl.pallas_call(
        functools.partial(_ffn_kernel, d_ff=d_ff, n_chunks=2, n_x_tiles=n_x_tiles, split_input=split_input,
                          final_norm=final_norm),
        grid=(B, nt),
        in_specs=row_specs + [
            pl.BlockSpec((None, None, None, 3, D), lambda b, t: (l, who(b, t), sub, 0, 0)),
            pl.BlockSpec((None, 1, D), lambda b, t: (l, 0, 0)),
            _resident((None, D, 2 * d_ff), lambda b, t: (l, 0, 0)),
            _resident((None, d_ff, D), lambda b, t: (l, 0, 0)),
        ] + extra_specs,
        out_specs=pl.BlockSpec((None, TOKEN_TILE, D), lambda b, t: (b, t, 0)),
        out_shape=jax.ShapeDtypeStruct((B, nt * TOKEN_TILE, D), F32),
        compiler_params=_params(2),
        name="ffn_half_step",
    )(*rows, mods, g, w_up, w_dn, *extra)


def _proj_kernel(r_ref, rp_ref, rn_ref, mod_ref, g_ref, wqk_ref, wb_ref, wf_ref, bf_ref, wc_ref,
                 qk_ref, zb_ref, zf_ref, hfull_ref, *, n_x_tiles, n_tiles):
    mod = mod_ref[...]
    g = g_ref[...]
    nm = lambda x: _norm_mod(x, g, mod[0:1], mod[1:2])
    t = pl.program_id(1)
    has_prev = jnp.logical_and(t != 0, t != n_x_tiles)
    has_next = jnp.logical_and(t != n_x_tiles - 1, t != n_tiles - 1)
    hm = nm(r_ref[...]).astype(BF16)
    hfull_ref[0:HALO, :] = jnp.where(has_prev, nm(rp_ref[...]), 0.0).astype(BF16)
    hfull_ref[HALO:HALO + TOKEN_TILE, :] = hm
    hfull_ref[HALO + TOKEN_TILE:, :] = jnp.where(has_next, nm(rn_ref[...]), 0.0).astype(BF16)
    hfull = hfull_ref[...]
    rows = TOKEN_TILE + 2 * HALO
    width = wqk_ref.shape[1]
    cw = 512
    for ci in range(width // cw):
        cs = slice(ci * cw, (ci + 1) * cw)
        z = _dot(hfull, wqk_ref[:, cs])
        w = wc_ref[:, cs]
        cv = w[0:1] * pltpu.roll(z, 1, 0) + w[1:2] * z + w[2:3] * pltpu.roll(z, rows - 1, 0)
        cv = cv[HALO:HALO + TOKEN_TILE]
        y = cv * jax.nn.sigmoid(cv)
        if ci * cw < width // 2:
            y = y * (M_HEAD_DIM ** -0.5)
        qk_ref[:, cs] = y.astype(BF16)
    zb_ref[...] = _dot(hm, wb_ref[...]).astype(BF16)
    zf_ref[...] = _dot(hm, wf_ref[...]) + bf_ref[...]


def _project(R, mods, g, wqk, wb, wf, bf, wconv, l, n_x_tiles):
    B, T, D = R.shape
    nt = T // TOKEN_TILE
    hb = TOKEN_TILE // HALO
    n_halo_blocks = T // HALO
    who = lambda b, t: jnp.where(t >= n_x_tiles, B, b)
    nqk, nb, nf = wqk.shape[2], wb.shape[2], wf.shape[2]
    return pl.pallas_call(
        functools.partial(_proj_kernel, n_x_tiles=n_x_tiles, n_tiles=nt),
        grid=(B, nt),
        in_specs=[
            pl.BlockSpec((None, TOKEN_TILE, D), lambda b, t: (b, t, 0)),
            pl.BlockSpec((None, HALO, D), lambda b, t: (b, jnp.maximum(t * hb - 1, 0), 0)),
            pl.BlockSpec((None, HALO, D), lambda b, t: (b, jnp.minimum((t + 1) * hb, n_halo_blocks - 1), 0)),
            pl.BlockSpec((None, None, None, 3, D), lambda b, t: (l, who(b, t), 1, 0, 0)),
            pl.BlockSpec((None, 1, D), lambda b, t: (l, 0, 0)),
            _resident((None, D, nqk), lambda b, t: (l, 0, 0)),
            _resident((None, D, nb), lambda b, t: (l, 0, 0)),
            _resident((None, D, nf), lambda b, t: (l, 0, 0)),
            pl.BlockSpec((None, 1, nf), lambda b, t: (l, 0, 0)),
            pl.BlockSpec((None, M_CONV, nqk), lambda b, t: (l, 0, 0)),
        ],
        out_specs=[
            pl.BlockSpec((None, TOKEN_TILE, nqk), lambda b, t: (b, t, 0)),
            pl.BlockSpec((None, TOKEN_TILE, nb), lambda b, t: (b, t, 0)),
            pl.BlockSpec((None, TOKEN_TILE, nf), lambda b, t: (b, t, 0)),
        ],
        out_shape=[
            jax.ShapeDtypeStruct((B, T, nqk), BF16),
            jax.ShapeDtypeStruct((B, T, nb), BF16),
            jax.ShapeDtypeStruct((B, T, nf), F32),
        ],
        scratch_shapes=[pltpu.VMEM((TOKEN_TILE + 2 * HALO, D), BF16)],
        compiler_params=_params(2),
        name="in_projection",
    )(R, R, R, mods, g, wqk, wb, wf, bf, wconv)


def _mlstm_kernel(q_ref, k_ref, v_ref, gr_ref, o_ref, c_ref, *, n_x, seq):
    L = TOKEN_TILE
    row = lax.broadcasted_iota(jnp.int32, (L, L), 0)
    col = lax.broadcasted_iota(jnp.int32, (L, L), 1)
    lower = col <= row
    upper = col >= row
    diag = col == row
    ones_rows = jnp.ones((MLSTM_AUG, L), BF16)
    log2e = math.log2(math.e)

    def step(d, c0, m, accumulate):
        feeds = upper if d == 0 else lower
        feeds_t = lower if d == 0 else upper
        sl = pl.ds(c0, L)
        q, k, v = q_ref[sl, :], k_ref[sl, :], v_ref[sl, :]
        qt = q.T
        vt = jnp.concatenate([v.T, ones_rows], axis=0)
        gr = gr_ref[:, sl]
        ig_row = gr[2 * d:2 * d + 1] * log2e
        lf_row = jax.nn.log_sigmoid(gr[2 * d + 1:2 * d + 2]) * log2e
        g_col = jnp.sum(jnp.where(diag, ig_row, 0.0) - jnp.where(feeds_t, lf_row, 0.0), axis=1, keepdims=True)
        lf8 = jnp.broadcast_to(lf_row, (8, L))
        lf_hi = lf8.astype(BF16)
        lf_lo = (lf8 - lf_hi.astype(F32)).astype(BF16)
        feeds_bf = jnp.where(feeds, 1.0, 0.0).astype(BF16)
        b_row = (_dot(lf_hi, feeds_bf) + _dot(lf_lo, feeds_bf))[0:1]
        b_end = jnp.sum(lf_row, axis=1, keepdims=True)
        g_row = ig_row - b_row
        g_masked = jnp.where(feeds, g_col, -jnp.inf)
        inter = b_row + m
        mj = jnp.maximum(inter, b_row + jnp.max(g_masked, axis=0, keepdims=True))
        st = _dot(k, qt) * jnp.exp2(g_masked + (b_row - mj))
        e_inter = jnp.exp2(inter - mj)
        c_old = c_ref[d]
        num = _dot(vt, st.astype(BF16)) + e_inter * _dot(c_old.astype(BF16), qt)
        den = num[M_HEAD_DIM:M_HEAD_DIM + 1]
        ht = num[:M_HEAD_DIM] * (1.0 / jnp.maximum(jnp.abs(den), jnp.exp2(-mj)))
        if accumulate:
            o_ref[sl, :] += ht.T
        else:
            o_ref[sl, :] = ht.T
        m_new = jnp.maximum(b_end + m, b_end + jnp.max(g_row, axis=1, keepdims=True))
        w_end = jnp.exp2(b_end + g_row - m_new)
        c_ref[d] = jnp.exp2(b_end + m - m_new) * c_old + _dot((vt.astype(F32) * w_end).astype(BF16), k)
        return m_new

    c_ref[...] = jnp.zeros(c_ref.shape, F32)
    m0 = jnp.zeros((1, 1), F32)
    mf = step(0, seq, m0, False)
    mb = step(1, seq, m0, True)

    def both(accumulate):
        def body(i, carry):
            mf, mb = carry
            for u in range(MLSTM_UNROLL):
                ci = i * MLSTM_UNROLL + u
                mf = step(0, pl.multiple_of(ci * L, L), mf, accumulate)
                mb = step(1, pl.multiple_of((n_x - 1 - ci) * L, L), mb, accumulate)
            return mf, mb
        return body

    trips = n_x // MLSTM_UNROLL
    carry = lax.fori_loop(0, trips // 2, both(False), (mf, mb))
    lax.fori_loop(trips // 2, trips, both(True), carry)


def _mlstm(QK, ZB, grow, seq):
    B, T, _ = QK.shape
    dh = M_HEAD_DIM
    n_x = seq // TOKEN_TILE
    return pl.pallas_call(
        functools.partial(_mlstm_kernel, n_x=n_x, seq=seq),
        grid=(B, M_HEADS),
        in_specs=[
            pl.BlockSpec((None, T, dh), lambda b, h: (b, 0, h)),
            pl.BlockSpec((None, T, dh), lambda b, h: (b, 0, M_HEADS + h)),
            pl.BlockSpec((None, T, dh), lambda b, h: (b, 0, h)),
            pl.BlockSpec((None, None, 4, T), lambda b, h: (b, h, 0, 0)),
        ],
        out_specs=pl.BlockSpec((None, T, dh), lambda b, h: (b, 0, h)),
        out_shape=jax.ShapeDtypeStruct((B, T, M_HEADS * dh), F32),
        scratch_shapes=[pltpu.VMEM((2, dh + MLSTM_AUG, dh), F32)],
        compiler_params=_params(2),
        name="mlstm_scan",
    )(QK, QK, ZB, grow)


def _mla_prep_kernel(zf_ref, cos_ref, sa_ref, sb_ref, gq_ref, gkv_ref, wuq_ref, wuk_ref, wuv_ref,
                     aq_ref, akt_ref, av_ref, *, scale):
    zf = zf_ref[...]
    cqn = _rms(zf[:, :A_QRANK], gq_ref[...]).astype(BF16)
    ckvn = _rms(zf[:, A_QRANK:ZF_KR], gkv_ref[...]).astype(BF16)
    cos, sa, sb = cos_ref[...], sa_ref[...], sb_ref[...]

    def rope(x):
        return x * cos + pltpu.roll(x, A_ROPE // 2, 1) * sa + pltpu.roll(x, LANE - A_ROPE // 2, 1) * sb

    qf = _dot(cqn, wuq_ref[...])
    for h in range(A_HEADS):
        o = h * A_QK
        aq_ref[:, o:o + A_NOPE] = (qf[:, o:o + A_NOPE] * scale).astype(BF16)
        aq_ref[:, o + A_NOPE:o + A_QK] = (rope(qf[:, o + A_NOPE:o + A_QK]) * scale).astype(BF16)
    kn = _dot(ckvn, wuk_ref[...])
    kr = rope(zf[:, ZF_KR:ZF_KR + LANE])
    for h in range(A_HEADS):
        kh = jnp.concatenate([kn[:, h * A_NOPE:(h + 1) * A_NOPE], kr], axis=1)
        akt_ref[h * A_QK:(h + 1) * A_QK, :] = kh.T.astype(BF16)
    av_ref[...] = _dot(ckvn, wuv_ref[...]).astype(BF16)


def _mla_prep(ZF, cos_t, sin_a, sin_b, gq, gkv, wuq, wuk, wuv, l):
    B, T, _ = ZF.shape
    nt = T // TOKEN_TILE
    scale = (A_NOPE + A_ROPE) ** -0.5 * math.log2(math.e)
    return pl.pallas_call(
        functools.partial(_mla_prep_kernel, scale=scale),
        grid=(B, nt),
        in_specs=[
            pl.BlockSpec((None, TOKEN_TILE, ZF_WIDTH), lambda b, t: (b, t, 0)),
            pl.BlockSpec((TOKEN_TILE, LANE), lambda b, t: (t, 0)),
            pl.BlockSpec((TOKEN_TILE, LANE), lambda b, t: (t, 0)),
            pl.BlockSpec((TOKEN_TILE, LANE), lambda b, t: (t, 0)),
            pl.BlockSpec((None, 1, A_QRANK), lambda b, t: (l, 0, 0)),
            pl.BlockSpec((None, 1, A_KVRANK), lambda b, t: (l, 0, 0)),
            _resident((None, A_QRANK, A_HEADS * A_QK), lambda b, t: (l, 0, 0)),
            _resident((None, A_KVRANK, A_HEADS * A_NOPE), lambda b, t: (l, 0, 0)),
            _resident((None, A_KVRANK, A_HEADS * A_VDIM), lambda b, t: (l, 0, 0)),
        ],
        out_specs=[
            pl.BlockSpec((None, TOKEN_TILE, A_HEADS * A_QK), lambda b, t: (b, t, 0)),
            pl.BlockSpec((None, A_HEADS * A_QK, TOKEN_TILE), lambda b, t: (b, 0, t)),
            pl.BlockSpec((None, TOKEN_TILE, A_HEADS * A_VDIM), lambda b, t: (b, t, 0)),
        ],
        out_shape=[
            jax.ShapeDtypeStruct((B, T, A_HEADS * A_QK), BF16),
            jax.ShapeDtypeStruct((B, A_HEADS * A_QK, T), BF16),
            jax.ShapeDtypeStruct((B, T, A_HEADS * A_VDIM), BF16),
        ],
        compiler_params=_params(2),
        name="mla_up_projection",
    )(ZF, cos_t, sin_a, sin_b, gq, gkv, wuq, wuk, wuv)


def _attn_kernel(q_ref, kt_ref, v_ref, o_ref, vaug_ref, *, n_split):
    nk = v_ref.shape[0]

    @pl.when(pl.program_id(2) == 0)
    def _():
        vaug_ref[:, :A_VDIM] = v_ref[...]
        lane = lax.broadcasted_iota(jnp.int32, (nk, LANE), 1)
        vaug_ref[:, A_VDIM:] = jnp.where(lane == 0, 1.0, 0.0).astype(BF16)

    rows = q_ref.shape[0] // n_split
    scores = [_dot(q_ref[r * rows:(r + 1) * rows, :], kt_ref[...]) for r in range(n_split)]
    for r, s in enumerate(scores):
        rs = slice(r * rows, (r + 1) * rows)
        p = jnp.exp2(s - jnp.max(s, axis=1, keepdims=True)).astype(BF16)
        acc = _dot(p, vaug_ref[...])
        o_ref[rs, :] = (acc[:, :A_VDIM] * (1.0 / acc[:, A_VDIM:A_VDIM + 1])).astype(o_ref.dtype)


def _attention(AQ, AKT, AV, seq):
    B, T, _ = AQ.shape
    ctx_len = T - seq
    tq = ATTN_Q_TILE
    width = A_HEADS * A_VDIM
    a_lat = pl.pallas_call(
        functools.partial(_attn_kernel, n_split=2),
        grid=(B, A_HEADS, seq // tq),
        in_specs=[
            pl.BlockSpec((None, tq, A_QK), lambda b, h, t: (b, t, h)),
            pl.BlockSpec((None, A_QK, T), lambda b, h, t: (b, h, 0)),
            pl.BlockSpec((None, T, A_VDIM), lambda b, h, t: (b, 0, h)),
        ],
        out_specs=pl.BlockSpec((None, tq, A_VDIM), lambda b, h, t: (b, t, h)),
        out_shape=jax.ShapeDtypeStruct((B, seq, width), BF16),
        scratch_shapes=[pltpu.VMEM((T, A_VDIM + LANE), BF16)],
        compiler_params=_params(3),
        name="mla_attention_latent",
    )(AQ, AKT, AV)
    cb = seq // ctx_len
    a_ctx = pl.pallas_call(
        functools.partial(_attn_kernel, n_split=1),
        grid=(B, A_HEADS, 1),
        in_specs=[
            pl.BlockSpec((None, ctx_len, A_QK), lambda b, h, t: (b, cb, h)),
            pl.BlockSpec((None, A_QK, ctx_len), lambda b, h, t: (b, h, cb)),
            pl.BlockSpec((None, ctx_len, A_VDIM), lambda b, h, t: (b, cb, h)),
        ],
        out_specs=pl.BlockSpec((None, ctx_len, A_VDIM), lambda b, h, t: (b, 0, h)),
        out_shape=jax.ShapeDtypeStruct((B, ctx_len, width), BF16),
        scratch_shapes=[pltpu.VMEM((ctx_len, A_VDIM + LANE), BF16)],
        compiler_params=_params(3),
        name="mla_attention_context",
    )(AQ, AKT, AV)
    return a_lat, a_ctx


def _merge_kernel(h_ref, mo_ref, bg_ref, al_ref, ac_ref, r_ref, mod_ref, gmh_ref, wbm_ref, wba_ref, wout_ref, o_ref,
                  *, n_x_tiles):
    a = jnp.where(pl.program_id(1) < n_x_tiles, al_ref[...], ac_ref[...])
    hh = h_ref[...]
    dh = M_HEAD_DIM
    hn = jnp.concatenate(
        [hh[:, i * dh:(i + 1) * dh]
         * lax.rsqrt(jnp.mean(hh[:, i * dh:(i + 1) * dh] * hh[:, i * dh:(i + 1) * dh], axis=-1, keepdims=True) + EPS)
         for i in range(M_HEADS)], axis=1) * gmh_ref[...]
    hm = (jax.nn.sigmoid(mo_ref[...].astype(F32)) * hn).astype(BF16)
    width = hh.shape[1]
    bg = bg_ref[...].astype(F32)
    u = jax.nn.sigmoid(bg[:, :width]) * _dot(hm, wbm_ref[...]) + jax.nn.sigmoid(bg[:, width:]) * _dot(a, wba_ref[...])
    y = _dot(u.astype(BF16), wout_ref[...])
    o_ref[...] = r_ref[...] + mod_ref[...][2:3] * y


def _merge(H, ZB, A_lat, A_ctx, R, mods, gmh, wbm, wba, wout, l, n_x_tiles):
    B, T, D = R.shape
    nt = T // TOKEN_TILE
    W = H.shape[2]
    AW = A_lat.shape[2]
    who = lambda b, t: jnp.where(t >= n_x_tiles, B, b)
    return pl.pallas_call(
        functools.partial(_merge_kernel, n_x_tiles=n_x_tiles),
        grid=(B, nt),
        in_specs=[
            pl.BlockSpec((None, TOKEN_TILE, W), lambda b, t: (b, t, 0)),
            pl.BlockSpec((None, TOKEN_TILE, W), lambda b, t: (b, t, 1)),
            pl.BlockSpec((None, TOKEN_TILE, 2 * D), lambda b, t: (b, t, 1)),
            pl.BlockSpec((None, TOKEN_TILE, AW), lambda b, t: (b, jnp.minimum(t, n_x_tiles - 1), 0)),
            pl.BlockSpec((None, TOKEN_TILE, AW), lambda b, t: (b, 0, 0)),
            pl.BlockSpec((None, TOKEN_TILE, D), lambda b, t: (b, t, 0)),
            pl.BlockSpec((None, None, None, 3, D), lambda b, t: (l, who(b, t), 1, 0, 0)),
            pl.BlockSpec((None, 1, W), lambda b, t: (l, 0, 0)),
            _resident((None, W, D), lambda b, t: (l, 0, 0)),
            _resident((None, AW, D), lambda b, t: (l, 0, 0)),
            _resident((None, D, D), lambda b, t: (l, 0, 0)),
        ],
        out_specs=pl.BlockSpec((None, TOKEN_TILE, D), lambda b, t: (b, t, 0)),
        out_shape=jax.ShapeDtypeStruct(R.shape, F32),
        compiler_params=_params(2),
        name="mixer_merge",
    )(H, ZB, ZB, A_lat, A_ctx, R, mods, gmh, wbm, wba, wout)


def _rope_tables(seq, ctx_len):
    half = A_ROPE // 2
    axis_rot = A_ROPE // 2
    pos = jnp.arange(seq)
    rowp = (pos // GRID_W).astype(F32)
    colp = (pos % GRID_W).astype(F32)
    inv = ROPE_THETA ** (-jnp.arange(0, axis_rot, 2, dtype=F32) / axis_rot)
    ang = jnp.concatenate([rowp[:, None] * inv, colp[:, None] * inv], axis=-1)
    cos, sin = jnp.cos(ang), jnp.sin(ang)
    zeros = jnp.zeros((seq, half), F32)
    pad = jnp.zeros((seq, LANE - A_ROPE), F32)
    cos_x = jnp.concatenate([cos, cos, pad], axis=1)
    sa_x = jnp.concatenate([zeros, sin, pad], axis=1)
    sb_x = jnp.concatenate([-sin, zeros, pad], axis=1)
    cos_c = jnp.concatenate([jnp.ones((ctx_len, A_ROPE), F32), jnp.zeros((ctx_len, LANE - A_ROPE), F32)], axis=1)
    zc = jnp.zeros((ctx_len, LANE), F32)
    return (jnp.concatenate([cos_x, cos_c], axis=0), jnp.concatenate([sa_x, zc], axis=0),
            jnp.concatenate([sb_x, zc], axis=0))


def kernel(x, c, ctx, c_ctx, w_ada, b_ada, g_n1, g_n2, g_n3, w_ff1_up, w_ff1_dn, w_ff2_up, w_ff2_dn, w_in, b_gate, w_conv, g_mh, g_qa, g_kva, w_uq, w_ukv, w_bm, w_ba, w_out, g_final):
    B, S, D = x.shape
    C = ctx.shape[1]
    L = w_ada.shape[0]
    T = S + C
    MW = M_HEADS * M_HEAD_DIM
    assert C == TOKEN_TILE and S % (2 * MLSTM_UNROLL * TOKEN_TILE) == 0 and S % ATTN_Q_TILE == 0 and S % GRID_W == 0
    assert B + 1 <= 8 and D == MW
    n_x_tiles = S // TOKEN_TILE

    perm = jnp.concatenate([jnp.arange(0, A_ROPE, 2), jnp.arange(1, A_ROPE, 2)])
    o_gate = 4 * MW
    o_cq = o_gate + 4 * M_HEADS
    o_ckv = o_cq + A_QRANK
    o_kr = o_ckv + A_KVRANK
    o_br = o_kr + A_ROPE
    wqk = w_in[:, :, :2 * MW].astype(BF16)
    wb = jnp.concatenate([w_in[:, :, 2 * MW:4 * MW], w_in[:, :, o_br:]], axis=2).astype(BF16)
    wf = jnp.concatenate([
        w_in[:, :, o_cq:o_kr], w_in[:, :, o_kr:o_br][:, :, perm], jnp.zeros((L, D, LANE - A_ROPE), F32),
        w_in[:, :, o_gate:o_cq], jnp.zeros((L, D, LANE - 4 * M_HEADS), F32)], axis=2).astype(BF16)
    bf = jnp.concatenate([jnp.zeros((L, ZF_GATE), F32), b_gate, jnp.zeros((L, LANE - 4 * M_HEADS), F32)],
                         axis=1).reshape(L, 1, ZF_WIDTH)
    dq = A_NOPE + A_ROPE
    wuq4 = w_uq.reshape(L, A_QRANK, A_HEADS, dq)
    wuq = jnp.concatenate([wuq4[..., :A_NOPE], wuq4[..., A_NOPE:][..., perm],
                           jnp.zeros((L, A_QRANK, A_HEADS, A_QK - dq), F32)], axis=-1)
    wuq = wuq.reshape(L, A_QRANK, A_HEADS * A_QK).astype(BF16)
    wukv4 = w_ukv.reshape(L, A_KVRANK, A_HEADS, A_NOPE + A_VDIM)
    wuk = wukv4[..., :A_NOPE].reshape(L, A_KVRANK, A_HEADS * A_NOPE).astype(BF16)
    wuv = wukv4[..., A_NOPE:].reshape(L, A_KVRANK, A_HEADS * A_VDIM).astype(BF16)
    wup1, wdn1 = w_ff1_up.astype(BF16), w_ff1_dn.astype(BF16)
    wup2, wdn2 = w_ff2_up.astype(BF16), w_ff2_dn.astype(BF16)
    wbm, wba, wout = w_bm.astype(BF16), w_ba.astype(BF16), w_out.astype(BF16)
    row3 = lambda a: a.reshape(L, 1, a.shape[1])
    gn1, gn2, gn3, gmh, gqa, gkva = map(row3, (g_n1, g_n2, g_n3, g_mh, g_qa, g_kva))
    cos_t, sin_a, sin_b = _rope_tables(S, C)

    cvec = jnp.concatenate([c, c_ctx[None, :], jnp.zeros((8 - B - 1, D), F32)], axis=0)
    mods = _modulations(cvec, w_ada, b_ada).reshape(L, 8, N_MOD // 3, 3, D)

    R = (x, ctx)
    for l in range(L):
        R = _ffn(R, mods, gn1, wup1, wdn1, l, 0, n_x_tiles)
        QK, ZB, ZF = _project(R, mods, gn2, wqk, wb, wf, bf, w_conv, l, n_x_tiles)
        gates = ZF[:, :, ZF_GATE:ZF_GATE + 4 * M_HEADS].reshape(B, T, 4, M_HEADS)
        grow = jnp.transpose(gates, (0, 3, 2, 1))
        H = _mlstm(QK, ZB, grow, S)
        AQ, AKT, AV = _mla_prep(ZF, cos_t, sin_a, sin_b, gqa, gkva, wuq, wuk, wuv, l)
        A_lat, A_ctx = _attention(AQ, AKT, AV, S)
        R = _merge(H, ZB, A_lat, A_ctx, R, mods, gmh, wbm, wba, wout, l, n_x_tiles)
        R = _ffn(R, mods, gn3, wup2, wdn2, l, 2, n_x_tiles, final_g=g_final if l == L - 1 else None)
    return R
```

```python
import functools
import math

import jax
import jax.numpy as jnp
from jax import lax
from jax.experimental import pallas as pl
from jax.experimental.pallas import tpu as pltpu

F32 = jnp.float32
BF16 = jnp.bfloat16
EPS = 1e-6

GRID_W = 64
ROPE_THETA = 10000.0
M_HEADS = 4
M_HEAD_DIM = 256
M_CONV = 3
A_HEADS = 8
A_NOPE = 128
A_ROPE = 64
A_VDIM = 128
A_QRANK = 384
A_KVRANK = 256
N_MOD = 9

TOKEN_TILE = 256
HALO = 16
MLSTM_AUG = 16
MLSTM_UNROLL = 2
ATTN_Q_TILE = 1024
LANE = 128
A_QK = 2 * LANE
ZF_WIDTH = A_QRANK + A_KVRANK + LANE
ZF_KR = A_QRANK + A_KVRANK
VMEM_LIMIT = 48 * 1024 * 1024


def _dot(a, b):
    return jnp.dot(a, b, preferred_element_type=F32)


def _rms(x, g):
    return x * lax.rsqrt(jnp.mean(x * x, axis=-1, keepdims=True) + EPS) * g


def _norm_mod(x, g, shift, scale):
    return _rms(x, g) * (1 + scale) + shift


def _params(n_axes, vmem=VMEM_LIMIT):
    return pltpu.CompilerParams(dimension_semantics=("arbitrary",) * n_axes, vmem_limit_bytes=vmem)


def _resident(block, index_map):
    return pl.BlockSpec(block, index_map, pipeline_mode=pl.Buffered(1))


def _mod_kernel(c_ref, w_ref, b_ref, o_ref):
    cv = c_ref[...]
    s = (cv * jax.nn.sigmoid(cv)).astype(BF16)
    o_ref[...] = _dot(s, w_ref[...].astype(BF16)) + b_ref[...]


def _modulations(cvec, w_ada, b_ada):
    L, D, N = w_ada.shape
    tn = 1024
    return pl.pallas_call(
        _mod_kernel,
        grid=(L, N // tn),
        in_specs=[
            pl.BlockSpec((8, D), lambda l, j: (0, 0)),
            pl.BlockSpec((None, D, tn), lambda l, j: (l, 0, j)),
            pl.BlockSpec((None, 1, tn), lambda l, j: (l, 0, j)),
        ],
        out_specs=pl.BlockSpec((None, 8, tn), lambda l, j: (l, 0, j)),
        out_shape=jax.ShapeDtypeStruct((L, 8, N), F32),
        compiler_params=_params(2),
        name="adaln_mod",
    )(cvec, w_ada, b_ada.reshape(L, 1, N))


def _ffn_kernel(*refs, d_ff, n_chunks, n_x_tiles, split_input, final_norm):
    refs = list(refs)
    o_ref = refs.pop()
    gf_ref = refs.pop() if final_norm else None
    if split_input:
        x = jnp.where(pl.program_id(1) < n_x_tiles, refs[0][...], refs[1][...])
        refs = refs[2:]
    else:
        x = refs[0][...]
        refs = refs[1:]
    mod_ref, g_ref, wup_ref, wdn_ref = refs
    mod = mod_ref[...]
    h = _norm_mod(x, g_ref[...], mod[0:1], mod[1:2]).astype(BF16)
    tf = d_ff // n_chunks
    acc = None
    for ci in range(n_chunks):
        a = _dot(h, wup_ref[:, ci * tf:(ci + 1) * tf])
        b = _dot(h, wup_ref[:, d_ff + ci * tf:d_ff + (ci + 1) * tf])
        act = (a * jax.nn.sigmoid(a) * b).astype(BF16)
        part = _dot(act, wdn_ref[ci * tf:(ci + 1) * tf, :])
        acc = part if acc is None else acc + part
    y = x + (0.5 * mod[2:3]) * acc
    o_ref[...] = _rms(y, gf_ref[...]) if final_norm else y


def _ffn(src, mods, g, w_up, w_dn, l, sub, n_x_tiles, final_g=None):
    split_input = isinstance(src, tuple)
    if split_input:
        xs, cs = src
        B, S, D = xs.shape
        T = S + cs.shape[1]
        row_specs = [
            pl.BlockSpec((None, TOKEN_TILE, D), lambda b, t: (b, jnp.minimum(t, n_x_tiles - 1), 0)),
            pl.BlockSpec((None, TOKEN_TILE, D), lambda b, t: (b, 0, 0)),
        ]
        rows = [xs, cs]
    else:
        B, T, D = src.shape
        row_specs = [pl.BlockSpec((None, TOKEN_TILE, D), lambda b, t: (b, t, 0))]
        rows = [src]
    d_ff = w_dn.shape[1]
    final_norm = final_g is not None
    nt = n_x_tiles if final_norm else T // TOKEN_TILE
    who = lambda b, t: jnp.where(t >= n_x_tiles, B, b)
    extra_specs = [pl.BlockSpec((1, D), lambda b, t: (0, 0))] if final_norm else []
    extra = [final_g.reshape(1, D)] if final_norm else []
    return pl.pallas_call(
        functools.partial(_ffn_kernel, d_ff=d_ff, n_chunks=2, n_x_tiles=n_x_tiles, split_input=split_input,
                          final_norm=final_norm),
        grid=(B, nt),
        in_specs=row_specs + [
            pl.BlockSpec((None, None, None, 3, D), lambda b, t: (l, who(b, t), sub, 0, 0)),
            pl.BlockSpec((None, 1, D), lambda b, t: (l, 0, 0)),
            _resident((None, D, 2 * d_ff), lambda b, t: (l, 0, 0)),
            _resident((None, d_ff, D), lambda b, t: (l, 0, 0)),
        ] + extra_specs,
        out_specs=pl.BlockSpec((None, TOKEN_TILE, D), lambda b, t: (b, t, 0)),
        out_shape=jax.ShapeDtypeStruct((B, nt * TOKEN_TILE, D), F32),
        compiler_params=_params(2),
        name="ffn_half_step",
    )(*rows, mods, g, w_up, w_dn, *extra)


def _proj_kernel(r_ref, rp_ref, rn_ref, mod_ref, g_ref, wqk_ref, wv_ref, wb_ref, wf_ref, wg_ref, bg_ref, wc_ref,
                 qt_ref, k_ref, vt_ref, zb_ref, zf_ref, gt_ref, hfull_ref, *, n_x_tiles, n_tiles):
    mod = mod_ref[...]
    g = g_ref[...]
    nm = lambda x: _norm_mod(x, g, mod[0:1], mod[1:2])
    t = pl.program_id(1)
    has_prev = jnp.logical_and(t != 0, t != n_x_tiles)
    has_next = jnp.logical_and(t != n_x_tiles - 1, t != n_tiles - 1)
    hm = nm(r_ref[...]).astype(BF16)
    hfull_ref[0:HALO, :] = jnp.where(has_prev, nm(rp_ref[...]), 0.0).astype(BF16)
    hfull_ref[HALO:HALO + TOKEN_TILE, :] = hm
    hfull_ref[HALO + TOKEN_TILE:, :] = jnp.where(has_next, nm(rn_ref[...]), 0.0).astype(BF16)
    hfull = hfull_ref[...]
    rows = TOKEN_TILE + 2 * HALO
    width = wqk_ref.shape[1]
    cw = 512
    for ci in range(width // cw):
        cs = slice(ci * cw, (ci + 1) * cw)
        z = _dot(hfull, wqk_ref[:, cs])
        w = wc_ref[:, cs]
        cv = w[0:1] * pltpu.roll(z, 1, 0) + w[1:2] * z + w[2:3] * pltpu.roll(z, rows - 1, 0)
        cv = cv[HALO:HALO + TOKEN_TILE]
        y = cv * jax.nn.sigmoid(cv)
        if ci * cw < width // 2:
            qt_ref[cs, :] = (y * (M_HEAD_DIM ** -0.5)).astype(BF16).T
        else:
            k_ref[:, ci * cw - width // 2:(ci + 1) * cw - width // 2] = y.astype(BF16)
    vt_ref[...] = _dot(hm, wv_ref[...]).astype(BF16).T
    zb_ref[...] = _dot(hm, wb_ref[...]).astype(BF16)
    zf_ref[...] = _dot(hm, wf_ref[...])
    gt_ref[...] = (_dot(hm, wg_ref[...]) + bg_ref[...]).T[0:gt_ref.shape[0], :]


def _project(R, mods, g, wqk, wv, wb, wf, wg, bg, wconv, l, n_x_tiles):
    B, T, D = R.shape
    nt = T // TOKEN_TILE
    hb = TOKEN_TILE // HALO
    n_halo_blocks = T // HALO
    who = lambda b, t: jnp.where(t >= n_x_tiles, B, b)
    nqk, nv, nb, nf, ng = wqk.shape[2], wv.shape[2], wb.shape[2], wf.shape[2], wg.shape[2]
    nq = nqk // 2
    n_gates = 4 * M_HEADS
    return pl.pallas_call(
        functools.partial(_proj_kernel, n_x_tiles=n_x_tiles, n_tiles=nt),
        grid=(B, nt),
        in_specs=[
            pl.BlockSpec((None, TOKEN_TILE, D), lambda b, t: (b, t, 0)),
            pl.BlockSpec((None, HALO, D), lambda b, t: (b, jnp.maximum(t * hb - 1, 0), 0)),
            pl.BlockSpec((None, HALO, D), lambda b, t: (b, jnp.minimum((t + 1) * hb, n_halo_blocks - 1), 0)),
            pl.BlockSpec((None, None, None, 3, D), lambda b, t: (l, who(b, t), 1, 0, 0)),
            pl.BlockSpec((None, 1, D), lambda b, t: (l, 0, 0)),
            _resident((None, D, nqk), lambda b, t: (l, 0, 0)),
            _resident((None, D, nv), lambda b, t: (l, 0, 0)),
            _resident((None, D, nb), lambda b, t: (l, 0, 0)),
            _resident((None, D, nf), lambda b, t: (l, 0, 0)),
            _resident((None, D, ng), lambda b, t: (l, 0, 0)),
            pl.BlockSpec((None, 1, ng), lambda b, t: (l, 0, 0)),
            pl.BlockSpec((None, M_CONV, nqk), lambda b, t: (l, 0, 0)),
        ],
        out_specs=[
            pl.BlockSpec((None, nq, TOKEN_TILE), lambda b, t: (b, 0, t)),
            pl.BlockSpec((None, TOKEN_TILE, nq), lambda b, t: (b, t, 0)),
            pl.BlockSpec((None, nv, TOKEN_TILE), lambda b, t: (b, 0, t)),
            pl.BlockSpec((None, TOKEN_TILE, nb), lambda b, t: (b, t, 0)),
            pl.BlockSpec((None, TOKEN_TILE, nf), lambda b, t: (b, t, 0)),
            pl.BlockSpec((None, n_gates, TOKEN_TILE), lambda b, t: (b, 0, t)),
        ],
        out_shape=[
            jax.ShapeDtypeStruct((B, nq, T), BF16),
            jax.ShapeDtypeStruct((B, T, nq), BF16),
            jax.ShapeDtypeStruct((B, nv, T), BF16),
            jax.ShapeDtypeStruct((B, T, nb), BF16),
            jax.ShapeDtypeStruct((B, T, nf), F32),
            jax.ShapeDtypeStruct((B, n_gates, T), F32),
        ],
        scratch_shapes=[pltpu.VMEM((TOKEN_TILE + 2 * HALO, D), BF16)],
        compiler_params=_params(2),
        name="in_projection",
    )(R, R, R, mods, g, wqk, wv, wb, wf, wg, bg, wconv)


def _mlstm_kernel(qt_ref, k_ref, vt_ref, gr_ref, o_ref, c_ref, pre_ref, *, n_x, seq):
    L = TOKEN_TILE
    row = lax.broadcasted_iota(jnp.int32, (L, L), 0)
    col = lax.broadcasted_iota(jnp.int32, (L, L), 1)
    lower = col <= row
    upper = col >= row
    diag = col == row
    ones_rows = jnp.ones((MLSTM_AUG, L), BF16)
    log2e = math.log2(math.e)

    def step(d, c0, m, accumulate):
        feeds = upper if d == 0 else lower
        sl = pl.ds(c0, L)
        qt, k = qt_ref[:, sl], k_ref[sl, :]
        vt = jnp.concatenate([vt_ref[:, sl], ones_rows], axis=0)
        b_row = pre_ref[2 * d, 0:1, sl]
        g_row = pre_ref[2 * d + 1, 0:1, sl]
        b_end = jnp.min(b_row, axis=1, keepdims=True)
        g_col = jnp.sum(jnp.where(diag, g_row, 0.0), axis=1, keepdims=True)
        g_masked = jnp.where(feeds, g_col, -jnp.inf)
        inter = b_row + m
        mj = jnp.maximum(inter, b_row + jnp.max(g_masked, axis=0, keepdims=True))
        st = _dot(k, qt) * jnp.exp2(g_masked + (b_row - mj))
        e_inter = jnp.exp2(inter - mj)
        c_old = c_ref[d]
        num = _dot(vt, st.astype(BF16)) + e_inter * _dot(c_old.astype(BF16), qt)
        den = num[M_HEAD_DIM:M_HEAD_DIM + 1]
        ht = num[:M_HEAD_DIM] * (1.0 / jnp.maximum(jnp.abs(den), jnp.exp2(-mj)))
        if accumulate:
            o_ref[:, sl] += ht
        else:
            o_ref[:, sl] = ht
        m_new = jnp.maximum(b_end + m, b_end + jnp.max(g_row, axis=1, keepdims=True))
        w_end = jnp.exp2(b_end + g_row - m_new)
        c_ref[d] = jnp.exp2(b_end + m - m_new) * c_old + _dot((vt.astype(F32) * w_end).astype(BF16), k)
        return m_new

    for c in range(pre_ref.shape[2] // L):
        cs = slice(c * L, (c + 1) * L)
        gr = gr_ref[:, cs]
        for d in range(2):
            lf8 = jnp.broadcast_to(jax.nn.log_sigmoid(gr[2 * d + 1:2 * d + 2]) * log2e, (8, L))
            lf_hi = lf8.astype(BF16)
            lf_lo = (lf8 - lf_hi.astype(F32)).astype(BF16)
            feeds_bf = jnp.where(upper if d == 0 else lower, 1.0, 0.0).astype(BF16)
            b8 = _dot(lf_hi, feeds_bf) + _dot(lf_lo, feeds_bf)
            pre_ref[2 * d, :, cs] = b8
            pre_ref[2 * d + 1, :, cs] = gr[2 * d:2 * d + 1] * log2e - b8

    c_ref[...] = jnp.zeros(c_ref.shape, F32)
    m0 = jnp.zeros((1, 1), F32)
    mf = step(0, seq, m0, False)
    mb = step(1, seq, m0, True)

    def both(accumulate):
        def body(i, carry):
            mf, mb = carry
            for u in range(MLSTM_UNROLL):
                ci = i * MLSTM_UNROLL + u
                mf = step(0, pl.multiple_of(ci * L, L), mf, accumulate)
                mb = step(1, pl.multiple_of((n_x - 1 - ci) * L, L), mb, accumulate)
            return mf, mb
        return body

    trips = n_x // MLSTM_UNROLL
    carry = lax.fori_loop(0, trips // 2, both(False), (mf, mb))
    lax.fori_loop(trips // 2, trips, both(True), carry)


def _mlstm(QT, K, VT, grow, seq):
    B, T, _ = K.shape
    dh = M_HEAD_DIM
    n_x = seq // TOKEN_TILE
    return pl.pallas_call(
        functools.partial(_mlstm_kernel, n_x=n_x, seq=seq),
        grid=(B, M_HEADS),
        in_specs=[
            pl.BlockSpec((None, dh, T), lambda b, h: (b, h, 0)),
            pl.BlockSpec((None, T, dh), lambda b, h: (b, 0, h)),
            pl.BlockSpec((None, dh, T), lambda b, h: (b, h, 0)),
            pl.BlockSpec((None, None, 4, T), lambda b, h: (b, h, 0, 0)),
        ],
        out_specs=pl.BlockSpec((None, dh, T), lambda b, h: (b, h, 0)),
        out_shape=jax.ShapeDtypeStruct((B, M_HEADS * dh, T), F32),
        scratch_shapes=[pltpu.VMEM((2, dh + MLSTM_AUG, dh), F32), pltpu.VMEM((4, 8, T), F32)],
        compiler_params=_params(2),
        name="mlstm_scan",
    )(QT, K, VT, grow)


def _mla_prep_kernel(zf_ref, cos_ref, sa_ref, sb_ref, gq_ref, gkv_ref, wuq_ref, wuk_ref, wuv_ref,
                     aq_ref, akt_ref, av_ref, *, scale):
    zf = zf_ref[...]
    cqn = _rms(zf[:, :A_QRANK], gq_ref[...]).astype(BF16)
    ckvn = _rms(zf[:, A_QRANK:ZF_KR], gkv_ref[...]).astype(BF16)
    cos, sa, sb = cos_ref[...], sa_ref[...], sb_ref[...]

    def rope(x):
        return x * cos + pltpu.roll(x, A_ROPE // 2, 1) * sa + pltpu.roll(x, LANE - A_ROPE // 2, 1) * sb

    qf = _dot(cqn, wuq_ref[...])
    for h in range(A_HEADS):
        o = h * A_QK
        aq_ref[:, o:o + A_NOPE] = (qf[:, o:o + A_NOPE] * scale).astype(BF16)
        aq_ref[:, o + A_NOPE:o + A_QK] = (rope(qf[:, o + A_NOPE:o + A_QK]) * scale).astype(BF16)
    kn = _dot(ckvn, wuk_ref[...])
    kr = rope(zf[:, ZF_KR:ZF_KR + LANE])
    for h in range(A_HEADS):
        kh = jnp.concatenate([kn[:, h * A_NOPE:(h + 1) * A_NOPE], kr], axis=1)
        akt_ref[h * A_QK:(h + 1) * A_QK, :] = kh.T.astype(BF16)
    av_ref[...] = _dot(ckvn, wuv_ref[...]).astype(BF16)


def _mla_prep(ZF, cos_t, sin_a, sin_b, gq, gkv, wuq, wuk, wuv, l):
    B, T, _ = ZF.shape
    nt = T // TOKEN_TILE
    scale = (A_NOPE + A_ROPE) ** -0.5 * math.log2(math.e)
    return pl.pallas_call(
        functools.partial(_mla_prep_kernel, scale=scale),
        grid=(B, nt),
        in_specs=[
            pl.BlockSpec((None, TOKEN_TILE, ZF_WIDTH), lambda b, t: (b, t, 0)),
            pl.BlockSpec((TOKEN_TILE, LANE), lambda b, t: (t, 0)),
            pl.BlockSpec((TOKEN_TILE, LANE), lambda b, t: (t, 0)),
            pl.BlockSpec((TOKEN_TILE, LANE), lambda b, t: (t, 0)),
            pl.BlockSpec((None, 1, A_QRANK), lambda b, t: (l, 0, 0)),
            pl.BlockSpec((None, 1, A_KVRANK), lambda b, t: (l, 0, 0)),
            _resident((None, A_QRANK, A_HEADS * A_QK), lambda b, t: (l, 0, 0)),
            _resident((None, A_KVRANK, A_HEADS * A_NOPE), lambda b, t: (l, 0, 0)),
            _resident((None, A_KVRANK, A_HEADS * A_VDIM), lambda b, t: (l, 0, 0)),
        ],
        out_specs=[
            pl.BlockSpec((None, TOKEN_TILE, A_HEADS * A_QK), lambda b, t: (b, t, 0)),
            pl.BlockSpec((None, A_HEADS * A_QK, TOKEN_TILE), lambda b, t: (b, 0, t)),
            pl.BlockSpec((None, TOKEN_TILE, A_HEADS * A_VDIM), lambda b, t: (b, t, 0)),
        ],
        out_shape=[
            jax.ShapeDtypeStruct((B, T, A_HEADS * A_QK), BF16),
            jax.ShapeDtypeStruct((B, A_HEADS * A_QK, T), BF16),
            jax.ShapeDtypeStruct((B, T, A_HEADS * A_VDIM), BF16),
        ],
        compiler_params=_params(2),
        name="mla_up_projection",
    )(ZF, cos_t, sin_a, sin_b, gq, gkv, wuq, wuk, wuv)


def _attn_kernel(q_ref, kt_ref, v_ref, o_ref, vaug_ref, *, n_split):
    nk = v_ref.shape[0]

    @pl.when(pl.program_id(2) == 0)
    def _():
        vaug_ref[:, :A_VDIM] = v_ref[...]
        lane = lax.broadcasted_iota(jnp.int32, (nk, LANE), 1)
        vaug_ref[:, A_VDIM:] = jnp.where(lane == 0, 1.0, 0.0).astype(BF16)

    rows = q_ref.shape[0] // n_split
    scores = [_dot(q_ref[r * rows:(r + 1) * rows, :], kt_ref[...]) for r in range(n_split)]
    for r, s in enumerate(scores):
        rs = slice(r * rows, (r + 1) * rows)
        p = jnp.exp2(s - jnp.max(s, axis=1, keepdims=True)).astype(BF16)
        acc = _dot(p, vaug_ref[...])
        o_ref[rs, :] = (acc[:, :A_VDIM] * (1.0 / acc[:, A_VDIM:A_VDIM + 1])).astype(o_ref.dtype)


def _attention(AQ, AKT, AV, seq):
    B, T, _ = AQ.shape
    ctx_len = T - seq
    tq = ATTN_Q_TILE
    width = A_HEADS * A_VDIM
    a_lat = pl.pallas_call(
        functools.partial(_attn_kernel, n_split=2),
        grid=(B, A_HEADS, seq // tq),
        in_specs=[
            pl.BlockSpec((None, tq, A_QK), lambda b, h, t: (b, t, h)),
            pl.BlockSpec((None, A_QK, T), lambda b, h, t: (b, h, 0)),
            pl.BlockSpec((None, T, A_VDIM), lambda b, h, t: (b, 0, h)),
        ],
        out_specs=pl.BlockSpec((None, tq, A_VDIM), lambda b, h, t: (b, t, h)),
        out_shape=jax.ShapeDtypeStruct((B, seq, width), BF16),
        scratch_shapes=[pltpu.VMEM((T, A_VDIM + LANE), BF16)],
        compiler_params=_params(3),
        name="mla_attention_latent",
    )(AQ, AKT, AV)
    cb = seq // ctx_len
    a_ctx = pl.pallas_call(
        functools.partial(_attn_kernel, n_split=1),
        grid=(B, A_HEADS, 1),
        in_specs=[
            pl.BlockSpec((None, ctx_len, A_QK), lambda b, h, t: (b, cb, h)),
            pl.BlockSpec((None, A_QK, ctx_len), lambda b, h, t: (b, h, cb)),
            pl.BlockSpec((None, ctx_len, A_VDIM), lambda b, h, t: (b, cb, h)),
        ],
        out_specs=pl.BlockSpec((None, ctx_len, A_VDIM), lambda b, h, t: (b, 0, h)),
        out_shape=jax.ShapeDtypeStruct((B, ctx_len, width), BF16),
        scratch_shapes=[pltpu.VMEM((ctx_len, A_VDIM + LANE), BF16)],
        compiler_params=_params(3),
        name="mla_attention_context",
    )(AQ, AKT, AV)
    return a_lat, a_ctx


def _merge_kernel(ht_ref, mo_ref, bg_ref, al_ref, ac_ref, r_ref, mod_ref, gmh_ref, wbm_ref, wba_ref, wout_ref, o_ref,
                  *, n_x_tiles):
    a = jnp.where(pl.program_id(1) < n_x_tiles, al_ref[...], ac_ref[...])
    dh = M_HEAD_DIM
    hn_t = jnp.concatenate(
        [ht_ref[i * dh:(i + 1) * dh, :]
         * lax.rsqrt(jnp.mean(ht_ref[i * dh:(i + 1) * dh, :] * ht_ref[i * dh:(i + 1) * dh, :], axis=0, keepdims=True) + EPS)
         for i in range(M_HEADS)], axis=0)
    hn = hn_t.T * gmh_ref[...]
    hm = (jax.nn.sigmoid(mo_ref[...].astype(F32)) * hn).astype(BF16)
    width = hn.shape[1]
    bg = bg_ref[...].astype(F32)
    u = jax.nn.sigmoid(bg[:, :width]) * _dot(hm, wbm_ref[...]) + jax.nn.sigmoid(bg[:, width:]) * _dot(a, wba_ref[...])
    y = _dot(u.astype(BF16), wout_ref[...])
    o_ref[...] = r_ref[...] + mod_ref[...][2:3] * y


def _merge(HT, ZB, A_lat, A_ctx, R, mods, gmh, wbm, wba, wout, l, n_x_tiles):
    B, T, D = R.shape
    nt = T // TOKEN_TILE
    W = HT.shape[1]
    AW = A_lat.shape[2]
    assert W == D
    who = lambda b, t: jnp.where(t >= n_x_tiles, B, b)
    return pl.pallas_call(
        functools.partial(_merge_kernel, n_x_tiles=n_x_tiles),
        grid=(B, nt),
        in_specs=[
            pl.BlockSpec((None, W, TOKEN_TILE), lambda b, t: (b, 0, t)),
            pl.BlockSpec((None, TOKEN_TILE, W), lambda b, t: (b, t, 2)),
            pl.BlockSpec((None, TOKEN_TILE, 2 * D), lambda b, t: (b, t, 0)),
            pl.BlockSpec((None, TOKEN_TILE, AW), lambda b, t: (b, jnp.minimum(t, n_x_tiles - 1), 0)),
            pl.BlockSpec((None, TOKEN_TILE, AW), lambda b, t: (b, 0, 0)),
            pl.BlockSpec((None, TOKEN_TILE, D), lambda b, t: (b, t, 0)),
            pl.BlockSpec((None, None, None, 3, D), lambda b, t: (l, who(b, t), 1, 0, 0)),
            pl.BlockSpec((None, 1, W), lambda b, t: (l, 0, 0)),
            _resident((None, W, D), lambda b, t: (l, 0, 0)),
            _resident((None, AW, D), lambda b, t: (l, 0, 0)),
            _resident((None, D, D), lambda b, t: (l, 0, 0)),
        ],
        out_specs=pl.BlockSpec((None, TOKEN_TILE, D), lambda b, t: (b, t, 0)),
        out_shape=jax.ShapeDtypeStruct(R.shape, F32),
        compiler_params=_params(2),
        name="mixer_merge",
    )(HT, ZB, ZB, A_lat, A_ctx, R, mods, gmh, wbm, wba, wout)


def _rope_tables(seq, ctx_len):
    half = A_ROPE // 2
    axis_rot = A_ROPE // 2
    pos = jnp.arange(seq)
    rowp = (pos // GRID_W).astype(F32)
    colp = (pos % GRID_W).astype(F32)
    inv = ROPE_THETA ** (-jnp.arange(0, axis_rot, 2, dtype=F32) / axis_rot)
    ang = jnp.concatenate([rowp[:, None] * inv, colp[:, None] * inv], axis=-1)
    cos, sin = jnp.cos(ang), jnp.sin(ang)
    zeros = jnp.zeros((seq, half), F32)
    pad = jnp.zeros((seq, LANE - A_ROPE), F32)
    cos_x = jnp.concatenate([cos, cos, pad], axis=1)
    sa_x = jnp.concatenate([zeros, sin, pad], axis=1)
    sb_x = jnp.concatenate([-sin, zeros, pad], axis=1)
    cos_c = jnp.concatenate([jnp.ones((ctx_len, A_ROPE), F32), jnp.zeros((ctx_len, LANE - A_ROPE), F32)], axis=1)
    zc = jnp.zeros((ctx_len, LANE), F32)
    return (jnp.concatenate([cos_x, cos_c], axis=0), jnp.concatenate([sa_x, zc], axis=0),
            jnp.concatenate([sb_x, zc], axis=0))


def kernel(x, c, ctx, c_ctx, w_ada, b_ada, g_n1, g_n2, g_n3, w_ff1_up, w_ff1_dn, w_ff2_up, w_ff2_dn, w_in, b_gate, w_conv, g_mh, g_qa, g_kva, w_uq, w_ukv, w_bm, w_ba, w_out, g_final):
    B, S, D = x.shape
    C = ctx.shape[1]
    L = w_ada.shape[0]
    T = S + C
    MW = M_HEADS * M_HEAD_DIM
    assert C == TOKEN_TILE and S % (2 * MLSTM_UNROLL * TOKEN_TILE) == 0 and S % ATTN_Q_TILE == 0 and S % GRID_W == 0
    assert B + 1 <= 8 and D == MW
    n_x_tiles = S // TOKEN_TILE

    perm = jnp.concatenate([jnp.arange(0, A_ROPE, 2), jnp.arange(1, A_ROPE, 2)])
    o_gate = 4 * MW
    o_cq = o_gate + 4 * M_HEADS
    o_ckv = o_cq + A_QRANK
    o_kr = o_ckv + A_KVRANK
    o_br = o_kr + A_ROPE
    n_gates = 4 * M_HEADS
    gate_perm = jnp.array([dk * M_HEADS + h for h in range(M_HEADS) for dk in range(4)])
    wqk = w_in[:, :, :2 * MW].astype(BF16)
    wv = w_in[:, :, 2 * MW:3 * MW].astype(BF16)
    wb = jnp.concatenate([w_in[:, :, o_br:], w_in[:, :, 3 * MW:4 * MW]], axis=2).astype(BF16)
    wf = jnp.concatenate([
        w_in[:, :, o_cq:o_kr], w_in[:, :, o_kr:o_br][:, :, perm], jnp.zeros((L, D, LANE - A_ROPE), F32)],
        axis=2).astype(BF16)
    wg = jnp.concatenate([w_in[:, :, o_gate:o_cq][:, :, gate_perm], jnp.zeros((L, D, LANE - n_gates), F32)],
                         axis=2).astype(BF16)
    bg = jnp.concatenate([b_gate[:, gate_perm], jnp.zeros((L, LANE - n_gates), F32)], axis=1).reshape(L, 1, LANE)
    dq = A_NOPE + A_ROPE
    wuq4 = w_uq.reshape(L, A_QRANK, A_HEADS, dq)
    wuq = jnp.concatenate([wuq4[..., :A_NOPE], wuq4[..., A_NOPE:][..., perm],
                           jnp.zeros((L, A_QRANK, A_HEADS, A_QK - dq), F32)], axis=-1)
    wuq = wuq.reshape(L, A_QRANK, A_HEADS * A_QK).astype(BF16)
    wukv4 = w_ukv.reshape(L, A_KVRANK, A_HEADS, A_NOPE + A_VDIM)
    wuk = wukv4[..., :A_NOPE].reshape(L, A_KVRANK, A_HEADS * A_NOPE).astype(BF16)
    wuv = wukv4[..., A_NOPE:].reshape(L, A_KVRANK, A_HEADS * A_VDIM).astype(BF16)
    wup1, wdn1 = w_ff1_up.astype(BF16), w_ff1_dn.astype(BF16)
    wup2, wdn2 = w_ff2_up.astype(BF16), w_ff2_dn.astype(BF16)
    wbm, wba, wout = w_bm.astype(BF16), w_ba.astype(BF16), w_out.astype(BF16)
    row3 = lambda a: a.reshape(L, 1, a.shape[1])
    gn1, gn2, gn3, gmh, gqa, gkva = map(row3, (g_n1, g_n2, g_n3, g_mh, g_qa, g_kva))
    cos_t, sin_a, sin_b = _rope_tables(S, C)

    cvec = jnp.concatenate([c, c_ctx[None, :], jnp.zeros((8 - B - 1, D), F32)], axis=0)
    mods = _modulations(cvec, w_ada, b_ada).reshape(L, 8, N_MOD // 3, 3, D)

    R = (x, ctx)
    for l in range(L):
        R = _ffn(R, mods, gn1, wup1, wdn1, l, 0, n_x_tiles)
        QT, K, VT, ZB, ZF, GT = _project(R, mods, gn2, wqk, wv, wb, wf, wg, bg, w_conv, l, n_x_tiles)
        HT = _mlstm(QT, K, VT, GT.reshape(B, M_HEADS, 4, T), S)
        AQ, AKT, AV = _mla_prep(ZF, cos_t, sin_a, sin_b, gqa, gkva, wuq, wuk, wuv, l)
        A_lat, A_ctx = _attention(AQ, AKT, AV, S)
        R = _merge(HT, ZB, A_lat, A_ctx, R, mods, gmh, wbm, wba, wout, l, n_x_tiles)
        R = _ffn(R, mods, gn3, wup2, wdn2, l, 2, n_x_tiles, final_g=g_final if l == L - 1 else None)
    return R
```

```python
import functools
import math

import jax
import jax.numpy as jnp
from jax import lax
from jax.experimental import pallas as pl
from jax.experimental.pallas import tpu as pltpu

F32 = jnp.float32
BF16 = jnp.bfloat16
EPS = 1e-6

GRID_W = 64
ROPE_THETA = 10000.0
M_HEADS = 4
M_HEAD_DIM = 256
M_CONV = 3
A_HEADS = 8
A_NOPE = 128
A_ROPE = 64
A_VDIM = 128
A_QRANK = 384
A_KVRANK = 256
N_MOD = 9

TOKEN_TILE = 256
HALO = 16
MLSTM_AUG = 16
MLSTM_UNROLL = 4
ATTN_Q_TILE = 1024
LANE = 128
A_QK = 2 * LANE
ZF_WIDTH = A_QRANK + A_KVRANK + LANE
ZF_KR = A_QRANK + A_KVRANK
VMEM_LIMIT = 48 * 1024 * 1024


def _dot(a, b):
    return jnp.dot(a, b, preferred_element_type=F32)


def _rms(x, g):
    return x * lax.rsqrt(jnp.mean(x * x, axis=-1, keepdims=True) + EPS) * g


def _norm_mod(x, g, shift, scale):
    return _rms(x, g) * (1 + scale) + shift


def _params(n_axes, vmem=VMEM_LIMIT):
    return pltpu.CompilerParams(dimension_semantics=("arbitrary",) * n_axes, vmem_limit_bytes=vmem)


def _resident(block, index_map):
    return pl.BlockSpec(block, index_map, pipeline_mode=pl.Buffered(1))


def _mod_kernel(c_ref, w_ref, b_ref, o_ref):
    cv = c_ref[...]
    s = (cv * jax.nn.sigmoid(cv)).astype(BF16)
    o_ref[...] = _dot(s, w_ref[...].astype(BF16)) + b_ref[...]


def _modulations(cvec, w_ada, b_ada):
    L, D, N = w_ada.shape
    tn = 1024
    return pl.pallas_call(
        _mod_kernel,
        grid=(L, N // tn),
        in_specs=[
            pl.BlockSpec((8, D), lambda l, j: (0, 0)),
            pl.BlockSpec((None, D, tn), lambda l, j: (l, 0, j)),
            pl.BlockSpec((None, 1, tn), lambda l, j: (l, 0, j)),
        ],
        out_specs=pl.BlockSpec((None, 8, tn), lambda l, j: (l, 0, j)),
        out_shape=jax.ShapeDtypeStruct((L, 8, N), F32),
        compiler_params=_params(2),
        name="adaln_mod",
    )(cvec, w_ada, b_ada.reshape(L, 1, N))


def _ffn_kernel(*refs, d_ff, n_chunks, n_x_tiles, split_input, final_norm):
    refs = list(refs)
    o_ref = refs.pop()
    gf_ref = refs.pop() if final_norm else None
    if split_input:
        x = jnp.where(pl.program_id(1) < n_x_tiles, refs[0][...], refs[1][...])
        refs = refs[2:]
    else:
        x = refs[0][...]
        refs = refs[1:]
    mod_ref, g_ref, wup_ref, wdn_ref = refs
    mod = mod_ref[...]
    h = _norm_mod(x, g_ref[...], mod[0:1], mod[1:2]).astype(BF16)
    tf = d_ff // n_chunks
    acc = None
    for ci in range(n_chunks):
        a = _dot(h, wup_ref[:, ci * tf:(ci + 1) * tf])
        b = _dot(h, wup_ref[:, d_ff + ci * tf:d_ff + (ci + 1) * tf])
        act = (a * jax.nn.sigmoid(a) * b).astype(BF16)
        part = _dot(act, wdn_ref[ci * tf:(ci + 1) * tf, :])
        acc = part if acc is None else acc + part
    y = x + (0.5 * mod[2:3]) * acc
    o_ref[...] = _rms(y, gf_ref[...]) if final_norm else y


def _ffn(src, mods, g, w_up, w_dn, l, sub, n_x_tiles, final_g=None):
    split_input = isinstance(src, tuple)
    if split_input:
        xs, cs = src
        B, S, D = xs.shape
        T = S + cs.shape[1]
        row_specs = [
            pl.BlockSpec((None, TOKEN_TILE, D), lambda b, t: (b, jnp.minimum(t, n_x_tiles - 1), 0)),
            pl.BlockSpec((None, TOKEN_TILE, D), lambda b, t: (b, 0, 0)),
        ]
        rows = [xs, cs]
    else:
        B, T, D = src.shape
        row_specs = [pl.BlockSpec((None, TOKEN_TILE, D), lambda b, t: (b, t, 0))]
        rows = [src]
    d_ff = w_dn.shape[1]
    final_norm = final_g is not None
    nt = n_x_tiles if final_norm else T // TOKEN_TILE
    who = lambda b, t: jnp.where(t >= n_x_tiles, B, b)
    extra_specs = [pl.BlockSpec((1, D), lambda b, t: (0, 0))] if final_norm else []
    extra = [final_g.reshape(1, D)] if final_norm else []
    return pl.pallas_call(
        functools.partial(_ffn_kernel, d_ff=d_ff, n_chunks=1, n_x_tiles=n_x_tiles, split_input=split_input,
                          final_norm=final_norm),
        grid=(B, nt),
        in_specs=row_specs + [
            pl.BlockSpec((None, None, None, 3, D), lambda b, t: (l, who(b, t), sub, 0, 0)),
            pl.BlockSpec((None, 1, D), lambda b, t: (l, 0, 0)),
            _resident((None, D, 2 * d_ff), lambda b, t: (l, 0, 0)),
            _resident((None, d_ff, D), lambda b, t: (l, 0, 0)),
        ] + extra_specs,
        out_specs=pl.BlockSpec((None, TOKEN_TILE, D), lambda b, t: (b, t, 0)),
        out_shape=jax.ShapeDtypeStruct((B, nt * TOKEN_TILE, D), F32),
        compiler_params=_params(2),
        name="ffn_half_step",
    )(*rows, mods, g, w_up, w_dn, *extra)


def _proj_kernel(r_ref, rp_ref, rn_ref, mod_ref, g_ref, wqk_ref, wv_ref, wb_ref, wf_ref, wg_ref, bg_ref, wc_ref,
                 qt_ref, k_ref, vt_ref, zb_ref, zf_ref, gt_ref, hfull_ref, *, n_x_tiles, n_tiles):
    mod = mod_ref[...]
    g = g_ref[...]
    nm = lambda x: _norm_mod(x, g, mod[0:1], mod[1:2])
    t = pl.program_id(1)
    has_prev = jnp.logical_and(t != 0, t != n_x_tiles)
    has_next = jnp.logical_and(t != n_x_tiles - 1, t != n_tiles - 1)
    hm = nm(r_ref[...]).astype(BF16)
    hfull_ref[0:HALO, :] = jnp.where(has_prev, nm(rp_ref[...]), 0.0).astype(BF16)
    hfull_ref[HALO:HALO + TOKEN_TILE, :] = hm
    hfull_ref[HALO + TOKEN_TILE:, :] = jnp.where(has_next, nm(rn_ref[...]), 0.0).astype(BF16)
    hfull = hfull_ref[...]
    rows = TOKEN_TILE + 2 * HALO
    width = wqk_ref.shape[1]
    cw = 512
    for ci in range(width // cw):
        cs = slice(ci * cw, (ci + 1) * cw)
        z = _dot(hfull, wqk_ref[:, cs])
        w = wc_ref[:, cs]
        cv = w[0:1] * pltpu.roll(z, 1, 0) + w[1:2] * z + w[2:3] * pltpu.roll(z, rows - 1, 0)
        cv = cv[HALO:HALO + TOKEN_TILE]
        y = cv * jax.nn.sigmoid(cv)
        if ci * cw < width // 2:
            qt_ref[cs, :] = (y * (M_HEAD_DIM ** -0.5)).astype(BF16).T
        else:
            k_ref[:, ci * cw - width // 2:(ci + 1) * cw - width // 2] = y.astype(BF16)
    vt_ref[...] = _dot(hm, wv_ref[...]).astype(BF16).T
    zb_ref[...] = _dot(hm, wb_ref[...]).astype(BF16)
    zf_ref[...] = _dot(hm, wf_ref[...])
    gt_ref[...] = (_dot(hm, wg_ref[...]) + bg_ref[...]).T[0:gt_ref.shape[0], :]


def _project(R, mods, g, wqk, wv, wb, wf, wg, bg, wconv, l, n_x_tiles):
    B, T, D = R.shape
    nt = T // TOKEN_TILE
    hb = TOKEN_TILE // HALO
    n_halo_blocks = T // HALO
    who = lambda b, t: jnp.where(t >= n_x_tiles, B, b)
    nqk, nv, nb, nf, ng = wqk.shape[2], wv.shape[2], wb.shape[2], wf.shape[2], wg.shape[2]
    nq = nqk // 2
    n_gates = 4 * M_HEADS
    return pl.pallas_call(
        functools.partial(_proj_kernel, n_x_tiles=n_x_tiles, n_tiles=nt),
        grid=(B, nt),
        in_specs=[
            pl.BlockSpec((None, TOKEN_TILE, D), lambda b, t: (b, t, 0)),
            pl.BlockSpec((None, HALO, D), lambda b, t: (b, jnp.maximum(t * hb - 1, 0), 0)),
            pl.BlockSpec((None, HALO, D), lambda b, t: (b, jnp.minimum((t + 1) * hb, n_halo_blocks - 1), 0)),
            pl.BlockSpec((None, None, None, 3, D), lambda b, t: (l, who(b, t), 1, 0, 0)),
            pl.BlockSpec((None, 1, D), lambda b, t: (l, 0, 0)),
            _resident((None, D, nqk), lambda b, t: (l, 0, 0)),
            _resident((None, D, nv), lambda b, t: (l, 0, 0)),
            _resident((None, D, nb), lambda b, t: (l, 0, 0)),
            _resident((None, D, nf), lambda b, t: (l, 0, 0)),
            _resident((None, D, ng), lambda b, t: (l, 0, 0)),
            pl.BlockSpec((None, 1, ng), lambda b, t: (l, 0, 0)),
            pl.BlockSpec((None, M_CONV, nqk), lambda b, t: (l, 0, 0)),
        ],
        out_specs=[
            pl.BlockSpec((None, nq, TOKEN_TILE), lambda b, t: (b, 0, t)),
            pl.BlockSpec((None, TOKEN_TILE, nq), lambda b, t: (b, t, 0)),
            pl.BlockSpec((None, nv, TOKEN_TILE), lambda b, t: (b, 0, t)),
            pl.BlockSpec((None, TOKEN_TILE, nb), lambda b, t: (b, t, 0)),
            pl.BlockSpec((None, TOKEN_TILE, nf), lambda b, t: (b, t, 0)),
            pl.BlockSpec((None, n_gates, TOKEN_TILE), lambda b, t: (b, 0, t)),
        ],
        out_shape=[
            jax.ShapeDtypeStruct((B, nq, T), BF16),
            jax.ShapeDtypeStruct((B, T, nq), BF16),
            jax.ShapeDtypeStruct((B, nv, T), BF16),
            jax.ShapeDtypeStruct((B, T, nb), BF16),
            jax.ShapeDtypeStruct((B, T, nf), F32),
            jax.ShapeDtypeStruct((B, n_gates, T), F32),
        ],
        scratch_shapes=[pltpu.VMEM((TOKEN_TILE + 2 * HALO, D), BF16)],
        compiler_params=_params(2),
        name="in_projection",
    )(R, R, R, mods, g, wqk, wv, wb, wf, wg, bg, wconv)


def _mlstm_kernel(qt_ref, k_ref, vt_ref, gr_ref, o_ref, c_ref, pre_ref, *, n_x, seq):
    L = TOKEN_TILE
    row = lax.broadcasted_iota(jnp.int32, (L, L), 0)
    col = lax.broadcasted_iota(jnp.int32, (L, L), 1)
    lower = col <= row
    upper = col >= row
    diag = col == row
    ones_rows = jnp.ones((MLSTM_AUG, L), BF16)
    log2e = math.log2(math.e)

    def step(d, c0, m, accumulate):
        feeds = upper if d == 0 else lower
        sl = pl.ds(c0, L)
        qt, k = qt_ref[:, sl], k_ref[sl, :]
        vt = jnp.concatenate([vt_ref[:, sl], ones_rows], axis=0)
        b_row = pre_ref[2 * d, 0:1, sl]
        g_row = pre_ref[2 * d + 1, 0:1, sl]
        b_end = jnp.min(b_row, axis=1, keepdims=True)
        g_col = jnp.sum(jnp.where(diag, g_row, 0.0), axis=1, keepdims=True)
        g_masked = jnp.where(feeds, g_col, -jnp.inf)
        inter = b_row + m
        mj = jnp.maximum(inter, b_row + jnp.max(g_masked, axis=0, keepdims=True))
        st = _dot(k, qt) * jnp.exp2(g_masked + (b_row - mj))
        e_inter = jnp.exp2(inter - mj)
        c_old = c_ref[d]
        num = _dot(vt, st.astype(BF16)) + e_inter * _dot(c_old.astype(BF16), qt)
        den = num[M_HEAD_DIM:M_HEAD_DIM + 1]
        ht = num[:M_HEAD_DIM] * (1.0 / jnp.maximum(jnp.abs(den), jnp.exp2(-mj)))
        if accumulate:
            o_ref[:, sl] += ht
        else:
            o_ref[:, sl] = ht
        m_new = jnp.maximum(b_end + m, b_end + jnp.max(g_row, axis=1, keepdims=True))
        w_end = jnp.exp2(b_end + g_row - m_new)
        c_ref[d] = jnp.exp2(b_end + m - m_new) * c_old + _dot((vt.astype(F32) * w_end).astype(BF16), k)
        return m_new

    for c in range(pre_ref.shape[2] // L):
        cs = slice(c * L, (c + 1) * L)
        gr = gr_ref[:, cs]
        for d in range(2):
            lf8 = jnp.broadcast_to(jax.nn.log_sigmoid(gr[2 * d + 1:2 * d + 2]) * log2e, (8, L))
            lf_hi = lf8.astype(BF16)
            lf_lo = (lf8 - lf_hi.astype(F32)).astype(BF16)
            feeds_bf = jnp.where(upper if d == 0 else lower, 1.0, 0.0).astype(BF16)
            b8 = _dot(lf_hi, feeds_bf) + _dot(lf_lo, feeds_bf)
            pre_ref[2 * d, :, cs] = b8
            pre_ref[2 * d + 1, :, cs] = gr[2 * d:2 * d + 1] * log2e - b8

    c_ref[...] = jnp.zeros(c_ref.shape, F32)
    m0 = jnp.zeros((1, 1), F32)
    mf = step(0, seq, m0, False)
    mb = step(1, seq, m0, True)

    def both(accumulate):
        def body(i, carry):
            mf, mb = carry
            for u in range(MLSTM_UNROLL):
                ci = i * MLSTM_UNROLL + u
                mf = step(0, pl.multiple_of(ci * L, L), mf, accumulate)
                mb = step(1, pl.multiple_of((n_x - 1 - ci) * L, L), mb, accumulate)
            return mf, mb
        return body

    trips = n_x // MLSTM_UNROLL
    carry = lax.fori_loop(0, trips // 2, both(False), (mf, mb))
    lax.fori_loop(trips // 2, trips, both(True), carry)


def _mlstm(QT, K, VT, grow, seq):
    B, T, _ = K.shape
    dh = M_HEAD_DIM
    n_x = seq // TOKEN_TILE
    return pl.pallas_call(
        functools.partial(_mlstm_kernel, n_x=n_x, seq=seq),
        grid=(B, M_HEADS),
        in_specs=[
            pl.BlockSpec((None, dh, T), lambda b, h: (b, h, 0)),
            pl.BlockSpec((None, T, dh), lambda b, h: (b, 0, h)),
            pl.BlockSpec((None, dh, T), lambda b, h: (b, h, 0)),
            pl.BlockSpec((None, None, 4, T), lambda b, h: (b, h, 0, 0)),
        ],
        out_specs=pl.BlockSpec((None, dh, T), lambda b, h: (b, h, 0)),
        out_shape=jax.ShapeDtypeStruct((B, M_HEADS * dh, T), F32),
        scratch_shapes=[pltpu.VMEM((2, dh + MLSTM_AUG, dh), F32), pltpu.VMEM((4, 8, T), F32)],
        compiler_params=_params(2),
        name="mlstm_scan",
    )(QT, K, VT, grow)


def _mla_prep_kernel(zf_ref, cos_ref, sa_ref, sb_ref, gq_ref, gkv_ref, wuq_ref, wuk_ref, wuv_ref,
                     aq_ref, ak_ref, av_ref, *, scale):
    zf = zf_ref[...]
    cqn = _rms(zf[:, :A_QRANK], gq_ref[...]).astype(BF16)
    ckvn = _rms(zf[:, A_QRANK:ZF_KR], gkv_ref[...]).astype(BF16)
    cos, sa, sb = cos_ref[...], sa_ref[...], sb_ref[...]

    def rope(x):
        return x * cos + pltpu.roll(x, A_ROPE // 2, 1) * sa + pltpu.roll(x, LANE - A_ROPE // 2, 1) * sb

    qf = _dot(cqn, wuq_ref[...])
    for h in range(A_HEADS):
        o = h * A_QK
        aq_ref[:, o:o + A_NOPE] = (qf[:, o:o + A_NOPE] * scale).astype(BF16)
        aq_ref[:, o + A_NOPE:o + A_QK] = (rope(qf[:, o + A_NOPE:o + A_QK]) * scale).astype(BF16)
    kn = _dot(ckvn, wuk_ref[...])
    kr = rope(zf[:, ZF_KR:ZF_KR + LANE])
    kr = kr.astype(BF16)
    for h in range(A_HEADS):
        ak_ref[:, h * A_QK:h * A_QK + A_NOPE] = kn[:, h * A_NOPE:(h + 1) * A_NOPE].astype(BF16)
        ak_ref[:, h * A_QK + A_NOPE:(h + 1) * A_QK] = kr
    av_ref[...] = _dot(ckvn, wuv_ref[...]).astype(BF16)


def _mla_prep(ZF, cos_t, sin_a, sin_b, gq, gkv, wuq, wuk, wuv, l):
    B, T, _ = ZF.shape
    nt = T // TOKEN_TILE
    scale = (A_NOPE + A_ROPE) ** -0.5 * math.log2(math.e)
    return pl.pallas_call(
        functools.partial(_mla_prep_kernel, scale=scale),
        grid=(B, nt),
        in_specs=[
            pl.BlockSpec((None, TOKEN_TILE, ZF_WIDTH), lambda b, t: (b, t, 0)),
            pl.BlockSpec((TOKEN_TILE, LANE), lambda b, t: (t, 0)),
            pl.BlockSpec((TOKEN_TILE, LANE), lambda b, t: (t, 0)),
            pl.BlockSpec((TOKEN_TILE, LANE), lambda b, t: (t, 0)),
            pl.BlockSpec((None, 1, A_QRANK), lambda b, t: (l, 0, 0)),
            pl.BlockSpec((None, 1, A_KVRANK), lambda b, t: (l, 0, 0)),
            _resident((None, A_QRANK, A_HEADS * A_QK), lambda b, t: (l, 0, 0)),
            _resident((None, A_KVRANK, A_HEADS * A_NOPE), lambda b, t: (l, 0, 0)),
            _resident((None, A_KVRANK, A_HEADS * A_VDIM), lambda b, t: (l, 0, 0)),
        ],
        out_specs=[
            pl.BlockSpec((None, TOKEN_TILE, A_HEADS * A_QK), lambda b, t: (b, t, 0)),
            pl.BlockSpec((None, TOKEN_TILE, A_HEADS * A_QK), lambda b, t: (b, t, 0)),
            pl.BlockSpec((None, TOKEN_TILE, A_HEADS * A_VDIM), lambda b, t: (b, t, 0)),
        ],
        out_shape=[
            jax.ShapeDtypeStruct((B, T, A_HEADS * A_QK), BF16),
            jax.ShapeDtypeStruct((B, T, A_HEADS * A_QK), BF16),
            jax.ShapeDtypeStruct((B, T, A_HEADS * A_VDIM), BF16),
        ],
        compiler_params=_params(2),
        name="mla_up_projection",
    )(ZF, cos_t, sin_a, sin_b, gq, gkv, wuq, wuk, wuv)


def _attn_kernel(q_ref, k_ref, v_ref, o_ref, kt_ref, vaug_ref, *, n_split):
    nk = v_ref.shape[0]

    @pl.when(pl.program_id(2) == 0)
    def _():
        for c in range(nk // TOKEN_TILE):
            cs = slice(c * TOKEN_TILE, (c + 1) * TOKEN_TILE)
            kt_ref[:, cs] = k_ref[cs, :].T
        vaug_ref[:, :A_VDIM] = v_ref[...]
        lane = lax.broadcasted_iota(jnp.int32, (nk, LANE), 1)
        vaug_ref[:, A_VDIM:] = jnp.where(lane == 0, 1.0, 0.0).astype(BF16)

    rows = q_ref.shape[0] // n_split
    scores = [_dot(q_ref[r * rows:(r + 1) * rows, :], kt_ref[...]) for r in range(n_split)]
    for r, s in enumerate(scores):
        rs = slice(r * rows, (r + 1) * rows)
        p = jnp.exp2(s - jnp.max(s, axis=1, keepdims=True)).astype(BF16)
        acc = _dot(p, vaug_ref[...])
        o_ref[rs, :] = (acc[:, :A_VDIM] * (1.0 / acc[:, A_VDIM:A_VDIM + 1])).astype(o_ref.dtype)


def _attention(AQ, AK, AV, seq, with_ctx):
    B, T, _ = AQ.shape
    ctx_len = T - seq
    tq = ATTN_Q_TILE
    width = A_HEADS * A_VDIM
    a_lat = pl.pallas_call(
        functools.partial(_attn_kernel, n_split=2),
        grid=(B, A_HEADS, seq // tq),
        in_specs=[
            pl.BlockSpec((None, tq, A_QK), lambda b, h, t: (b, t, h)),
            pl.BlockSpec((None, T, A_QK), lambda b, h, t: (b, 0, h)),
            pl.BlockSpec((None, T, A_VDIM), lambda b, h, t: (b, 0, h)),
        ],
        out_specs=pl.BlockSpec((None, tq, A_VDIM), lambda b, h, t: (b, t, h)),
        out_shape=jax.ShapeDtypeStruct((B, seq, width), BF16),
        scratch_shapes=[pltpu.VMEM((A_QK, T), BF16), pltpu.VMEM((T, A_VDIM + LANE), BF16)],
        compiler_params=_params(3),
        name="mla_attention_latent",
    )(AQ, AK, AV)
    if not with_ctx:
        return a_lat, None
    cb = seq // ctx_len
    a_ctx = pl.pallas_call(
        functools.partial(_attn_kernel, n_split=1),
        grid=(B, A_HEADS, 1),
        in_specs=[
            pl.BlockSpec((None, ctx_len, A_QK), lambda b, h, t: (b, cb, h)),
            pl.BlockSpec((None, ctx_len, A_QK), lambda b, h, t: (b, cb, h)),
            pl.BlockSpec((None, ctx_len, A_VDIM), lambda b, h, t: (b, cb, h)),
        ],
        out_specs=pl.BlockSpec((None, ctx_len, A_VDIM), lambda b, h, t: (b, 0, h)),
        out_shape=jax.ShapeDtypeStruct((B, ctx_len, width), BF16),
        scratch_shapes=[pltpu.VMEM((A_QK, ctx_len), BF16), pltpu.VMEM((ctx_len, A_VDIM + LANE), BF16)],
        compiler_params=_params(3),
        name="mla_attention_context",
    )(AQ, AK, AV)
    return a_lat, a_ctx


def _merge_kernel(ht_ref, mo_ref, bg_ref, al_ref, ac_ref, r_ref, mod_ref, gmh_ref, wbm_ref, wba_ref, wout_ref, o_ref,
                  *, n_x_tiles):
    a = jnp.where(pl.program_id(1) < n_x_tiles, al_ref[...], ac_ref[...])
    dh = M_HEAD_DIM
    hn_t = jnp.concatenate(
        [ht_ref[i * dh:(i + 1) * dh, :]
         * lax.rsqrt(jnp.mean(ht_ref[i * dh:(i + 1) * dh, :] * ht_ref[i * dh:(i + 1) * dh, :], axis=0, keepdims=True) + EPS)
         for i in range(M_HEADS)], axis=0)
    hn = hn_t.T * gmh_ref[...]
    hm = (jax.nn.sigmoid(mo_ref[...].astype(F32)) * hn).astype(BF16)
    width = hn.shape[1]
    bg = bg_ref[...].astype(F32)
    u = jax.nn.sigmoid(bg[:, :width]) * _dot(hm, wbm_ref[...]) + jax.nn.sigmoid(bg[:, width:]) * _dot(a, wba_ref[...])
    y = _dot(u.astype(BF16), wout_ref[...])
    o_ref[...] = r_ref[...] + mod_ref[...][2:3] * y


def _merge(HT, ZB, A_lat, A_ctx, R, mods, gmh, wbm, wba, wout, l, n_x_tiles, latent_only):
    B, T, D = R.shape
    nt = n_x_tiles if latent_only else T // TOKEN_TILE
    W = HT.shape[1]
    AW = A_lat.shape[2]
    assert W == D
    who = lambda b, t: jnp.where(t >= n_x_tiles, B, b)
    return pl.pallas_call(
        functools.partial(_merge_kernel, n_x_tiles=n_x_tiles),
        grid=(B, nt),
        in_specs=[
            pl.BlockSpec((None, W, TOKEN_TILE), lambda b, t: (b, 0, t)),
            pl.BlockSpec((None, TOKEN_TILE, W), lambda b, t: (b, t, 2)),
            pl.BlockSpec((None, TOKEN_TILE, 2 * D), lambda b, t: (b, t, 0)),
            pl.BlockSpec((None, TOKEN_TILE, AW), lambda b, t: (b, jnp.minimum(t, n_x_tiles - 1), 0)),
            pl.BlockSpec((None, TOKEN_TILE, AW), lambda b, t: (b, 0, 0)),
            pl.BlockSpec((None, TOKEN_TILE, D), lambda b, t: (b, t, 0)),
            pl.BlockSpec((None, None, None, 3, D), lambda b, t: (l, who(b, t), 1, 0, 0)),
            pl.BlockSpec((None, 1, W), lambda b, t: (l, 0, 0)),
            _resident((None, W, D), lambda b, t: (l, 0, 0)),
            _resident((None, AW, D), lambda b, t: (l, 0, 0)),
            _resident((None, D, D), lambda b, t: (l, 0, 0)),
        ],
        out_specs=pl.BlockSpec((None, TOKEN_TILE, D), lambda b, t: (b, t, 0)),
        out_shape=jax.ShapeDtypeStruct((B, nt * TOKEN_TILE, D), F32),
        compiler_params=_params(2),
        name="mixer_merge",
    )(HT, ZB, ZB, A_lat, A_ctx, R, mods, gmh, wbm, wba, wout)


def _rope_tables(seq, ctx_len):
    half = A_ROPE // 2
    axis_rot = A_ROPE // 2
    pos = jnp.arange(seq)
    rowp = (pos // GRID_W).astype(F32)
    colp = (pos % GRID_W).astype(F32)
    inv = ROPE_THETA ** (-jnp.arange(0, axis_rot, 2, dtype=F32) / axis_rot)
    ang = jnp.concatenate([rowp[:, None] * inv, colp[:, None] * inv], axis=-1)
    cos, sin = jnp.cos(ang), jnp.sin(ang)
    zeros = jnp.zeros((seq, half), F32)
    pad = jnp.zeros((seq, LANE - A_ROPE), F32)
    cos_x = jnp.concatenate([cos, cos, pad], axis=1)
    sa_x = jnp.concatenate([zeros, sin, pad], axis=1)
    sb_x = jnp.concatenate([-sin, zeros, pad], axis=1)
    cos_c = jnp.concatenate([jnp.ones((ctx_len, A_ROPE), F32), jnp.zeros((ctx_len, LANE - A_ROPE), F32)], axis=1)
    zc = jnp.zeros((ctx_len, LANE), F32)
    return (jnp.concatenate([cos_x, cos_c], axis=0), jnp.concatenate([sa_x, zc], axis=0),
            jnp.concatenate([sb_x, zc], axis=0))


def kernel(x, c, ctx, c_ctx, w_ada, b_ada, g_n1, g_n2, g_n3, w_ff1_up, w_ff1_dn, w_ff2_up, w_ff2_dn, w_in, b_gate, w_conv, g_mh, g_qa, g_kva, w_uq, w_ukv, w_bm, w_ba, w_out, g_final):
    B, S, D = x.shape
    C = ctx.shape[1]
    L = w_ada.shape[0]
    T = S + C
    MW = M_HEADS * M_HEAD_DIM
    assert C == TOKEN_TILE and S % (2 * MLSTM_UNROLL * TOKEN_TILE) == 0 and S % ATTN_Q_TILE == 0 and S % GRID_W == 0
    assert B + 1 <= 8 and D == MW
    n_x_tiles = S // TOKEN_TILE

    perm = jnp.concatenate([jnp.arange(0, A_ROPE, 2), jnp.arange(1, A_ROPE, 2)])
    o_gate = 4 * MW
    o_cq = o_gate + 4 * M_HEADS
    o_ckv = o_cq + A_QRANK
    o_kr = o_ckv + A_KVRANK
    o_br = o_kr + A_ROPE
    n_gates = 4 * M_HEADS
    gate_perm = jnp.array([dk * M_HEADS + h for h in range(M_HEADS) for dk in range(4)])
    wqk = w_in[:, :, :2 * MW].astype(BF16)
    wv = w_in[:, :, 2 * MW:3 * MW].astype(BF16)
    wb = jnp.concatenate([w_in[:, :, o_br:], w_in[:, :, 3 * MW:4 * MW]], axis=2).astype(BF16)
    wf = jnp.concatenate([
        w_in[:, :, o_cq:o_kr], w_in[:, :, o_kr:o_br][:, :, perm], jnp.zeros((L, D, LANE - A_ROPE), F32)],
        axis=2).astype(BF16)
    wg = jnp.concatenate([w_in[:, :, o_gate:o_cq][:, :, gate_perm], jnp.zeros((L, D, LANE - n_gates), F32)],
                         axis=2).astype(BF16)
    bg = jnp.concatenate([b_gate[:, gate_perm], jnp.zeros((L, LANE - n_gates), F32)], axis=1).reshape(L, 1, LANE)
    dq = A_NOPE + A_ROPE
    wuq4 = w_uq.reshape(L, A_QRANK, A_HEADS, dq)
    wuq = jnp.concatenate([wuq4[..., :A_NOPE], wuq4[..., A_NOPE:][..., perm],
                           jnp.zeros((L, A_QRANK, A_HEADS, A_QK - dq), F32)], axis=-1)
    wuq = wuq.reshape(L, A_QRANK, A_HEADS * A_QK).astype(BF16)
    wukv4 = w_ukv.reshape(L, A_KVRANK, A_HEADS, A_NOPE + A_VDIM)
    wuk = wukv4[..., :A_NOPE].reshape(L, A_KVRANK, A_HEADS * A_NOPE).astype(BF16)
    wuv = wukv4[..., A_NOPE:].reshape(L, A_KVRANK, A_HEADS * A_VDIM).astype(BF16)
    wup1, wdn1 = w_ff1_up.astype(BF16), w_ff1_dn.astype(BF16)
    wup2, wdn2 = w_ff2_up.astype(BF16), w_ff2_dn.astype(BF16)
    wbm, wba, wout = w_bm.astype(BF16), w_ba.astype(BF16), w_out.astype(BF16)
    row3 = lambda a: a.reshape(L, 1, a.shape[1])
    gn1, gn2, gn3, gmh, gqa, gkva = map(row3, (g_n1, g_n2, g_n3, g_mh, g_qa, g_kva))
    cos_t, sin_a, sin_b = _rope_tables(S, C)

    cvec = jnp.concatenate([c, c_ctx[None, :], jnp.zeros((8 - B - 1, D), F32)], axis=0)
    mods = _modulations(cvec, w_ada, b_ada).reshape(L, 8, N_MOD // 3, 3, D)

    R = (x, ctx)
    for l in range(L):
        R = _ffn(R, mods, gn1, wup1, wdn1, l, 0, n_x_tiles)
        QT, K, VT, ZB, ZF, GT = _project(R, mods, gn2, wqk, wv, wb, wf, wg, bg, w_conv, l, n_x_tiles)
        HT = _mlstm(QT, K, VT, GT.reshape(B, M_HEADS, 4, T), S)
        AQ, AK, AV = _mla_prep(ZF, cos_t, sin_a, sin_b, gqa, gkva, wuq, wuk, wuv, l)
        last = l == L - 1
        A_lat, A_ctx = _attention(AQ, AK, AV, S, with_ctx=not last)
        R = _merge(HT, ZB, A_lat, A_lat if last else A_ctx, R, mods, gmh, wbm, wba, wout, l, n_x_tiles, latent_only=last)
        R = _ffn(R, mods, gn3, wup2, wdn2, l, 2, n_x_tiles, final_g=g_final if last else None)
    return R
```

```python
import functools
import math

import jax
import jax.numpy as jnp
from jax import lax
from jax.experimental import pallas as pl
from jax.experimental.pallas import tpu as pltpu

F32 = jnp.float32
BF16 = jnp.bfloat16
EPS = 1e-6

GRID_W = 64
ROPE_THETA = 10000.0
M_HEADS = 4
M_HEAD_DIM = 256
M_CONV = 3
A_HEADS = 8
A_NOPE = 128
A_ROPE = 64
A_VDIM = 128
A_QRANK = 384
A_KVRANK = 256
N_MOD = 9

TOKEN_TILE = 256
HALO = 16
MLSTM_AUG = 16
FFN_BATCH_TILE = 2
MLSTM_UNROLL = 4
ATTN_Q_TILE = 1024
LANE = 128
A_QK = 2 * LANE
ZF_WIDTH = A_QRANK + A_KVRANK + LANE
ZF_KR = A_QRANK + A_KVRANK
VMEM_LIMIT = 48 * 1024 * 1024


def _dot(a, b):
    return jnp.dot(a, b, preferred_element_type=F32)


def _rms(x, g):
    return x * lax.rsqrt(jnp.mean(x * x, axis=-1, keepdims=True) + EPS) * g


def _norm_mod(x, g, shift, scale):
    return _rms(x, g) * (1 + scale) + shift


def _params(n_axes, vmem=VMEM_LIMIT):
    return pltpu.CompilerParams(dimension_semantics=("arbitrary",) * n_axes, vmem_limit_bytes=vmem)


def _resident(block, index_map):
    return pl.BlockSpec(block, index_map, pipeline_mode=pl.Buffered(1))


def _mod_kernel(c_ref, w_ref, b_ref, o_ref):
    cv = c_ref[...]
    s = (cv * jax.nn.sigmoid(cv)).astype(BF16)
    o_ref[...] = _dot(s, w_ref[...].astype(BF16)) + b_ref[...]


def _modulations(cvec, w_ada, b_ada):
    L, D, N = w_ada.shape
    tn = 1024
    return pl.pallas_call(
        _mod_kernel,
        grid=(L, N // tn),
        in_specs=[
            pl.BlockSpec((8, D), lambda l, j: (0, 0)),
            pl.BlockSpec((None, D, tn), lambda l, j: (l, 0, j)),
            pl.BlockSpec((None, 1, tn), lambda l, j: (l, 0, j)),
        ],
        out_specs=pl.BlockSpec((None, 8, tn), lambda l, j: (l, 0, j)),
        out_shape=jax.ShapeDtypeStruct((L, 8, N), F32),
        compiler_params=_params(2),
        name="adaln_mod",
    )(cvec, w_ada, b_ada.reshape(L, 1, N))


def _ffn_kernel(*refs, d_ff, n_chunks, n_x_tiles, split_input, final_norm):
    refs = list(refs)
    o_ref = refs.pop()
    gf_ref = refs.pop() if final_norm else None
    if split_input:
        x = jnp.where(pl.program_id(1) < n_x_tiles, refs[0][...], refs[1][...])
        refs = refs[2:]
    else:
        x = refs[0][...]
        refs = refs[1:]
    mod_ref, g_ref, wup_ref, wdn_ref = refs
    mod = mod_ref[...]
    nb, rows, dm = x.shape
    h = _norm_mod(x, g_ref[...], mod[:, 0:1], mod[:, 1:2]).astype(BF16).reshape(nb * rows, dm)
    tf = d_ff // n_chunks
    acc = None
    for ci in range(n_chunks):
        a = _dot(h, wup_ref[:, ci * tf:(ci + 1) * tf])
        b = _dot(h, wup_ref[:, d_ff + ci * tf:d_ff + (ci + 1) * tf])
        act = (a * jax.nn.sigmoid(a) * b).astype(BF16)
        part = _dot(act, wdn_ref[ci * tf:(ci + 1) * tf, :])
        acc = part if acc is None else acc + part
    y = x + (0.5 * mod[:, 2:3]) * acc.reshape(nb, rows, dm)
    o_ref[...] = _rms(y, gf_ref[...]) if final_norm else y


def _ffn(src, mods, g, w_up, w_dn, l, sub, n_x_tiles, final_g=None):
    split_input = isinstance(src, tuple)
    nbt = FFN_BATCH_TILE
    if split_input:
        xs, cs = src
        B, S, D = xs.shape
        T = S + cs.shape[1]
        row_specs = [
            pl.BlockSpec((nbt, TOKEN_TILE, D), lambda b, t: (b, jnp.minimum(t, n_x_tiles - 1), 0)),
            pl.BlockSpec((nbt, TOKEN_TILE, D), lambda b, t: (b, 0, 0)),
        ]
        rows = [xs, cs]
    else:
        B, T, D = src.shape
        row_specs = [pl.BlockSpec((nbt, TOKEN_TILE, D), lambda b, t: (b, t, 0))]
        rows = [src]
    d_ff = w_dn.shape[1]
    final_norm = final_g is not None
    nt = n_x_tiles if final_norm else T // TOKEN_TILE
    who = lambda b, t: jnp.where(t >= n_x_tiles, B // nbt, b)
    extra_specs = [pl.BlockSpec((1, D), lambda b, t: (0, 0))] if final_norm else []
    extra = [final_g.reshape(1, D)] if final_norm else []
    return pl.pallas_call(
        functools.partial(_ffn_kernel, d_ff=d_ff, n_chunks=1, n_x_tiles=n_x_tiles, split_input=split_input,
                          final_norm=final_norm),
        grid=(B // nbt, nt),
        in_specs=row_specs + [
            pl.BlockSpec((None, nbt, None, 3, D), lambda b, t: (l, who(b, t), sub, 0, 0)),
            pl.BlockSpec((None, 1, D), lambda b, t: (l, 0, 0)),
            _resident((None, D, 2 * d_ff), lambda b, t: (l, 0, 0)),
            _resident((None, d_ff, D), lambda b, t: (l, 0, 0)),
        ] + extra_specs,
        out_specs=pl.BlockSpec((nbt, TOKEN_TILE, D), lambda b, t: (b, t, 0)),
        out_shape=jax.ShapeDtypeStruct((B, nt * TOKEN_TILE, D), F32),
        compiler_params=_params(2),
        name="ffn_half_step",
    )(*rows, mods, g, w_up, w_dn, *extra)


def _proj_kernel(r_ref, rp_ref, rn_ref, mod_ref, g_ref, wqk_ref, wv_ref, wb_ref, wf_ref, wg_ref, bg_ref, wc_ref,
                 qt_ref, k_ref, vt_ref, zb_ref, zf_ref, gt_ref, hfull_ref, *, n_x_tiles, n_tiles):
    mod = mod_ref[...]
    g = g_ref[...]
    nm = lambda x: _norm_mod(x, g, mod[0:1], mod[1:2])
    t = pl.program_id(1)
    has_prev = jnp.logical_and(t != 0, t != n_x_tiles)
    has_next = jnp.logical_and(t != n_x_tiles - 1, t != n_tiles - 1)
    hm = nm(r_ref[...]).astype(BF16)
    hfull_ref[0:HALO, :] = jnp.where(has_prev, nm(rp_ref[...]), 0.0).astype(BF16)
    hfull_ref[HALO:HALO + TOKEN_TILE, :] = hm
    hfull_ref[HALO + TOKEN_TILE:, :] = jnp.where(has_next, nm(rn_ref[...]), 0.0).astype(BF16)
    hfull = hfull_ref[...]
    rows = TOKEN_TILE + 2 * HALO
    width = wqk_ref.shape[1]
    cw = 512
    for ci in range(width // cw):
        cs = slice(ci * cw, (ci + 1) * cw)
        z = _dot(hfull, wqk_ref[:, cs])
        w = wc_ref[:, cs]
        cv = w[0:1] * pltpu.roll(z, 1, 0) + w[1:2] * z + w[2:3] * pltpu.roll(z, rows - 1, 0)
        cv = cv[HALO:HALO + TOKEN_TILE]
        y = cv * jax.nn.sigmoid(cv)
        if ci * cw < width // 2:
            qt_ref[cs, :] = (y * (M_HEAD_DIM ** -0.5)).astype(BF16).T
        else:
            k_ref[:, ci * cw - width // 2:(ci + 1) * cw - width // 2] = y.astype(BF16)
    vt_ref[...] = _dot(hm, wv_ref[...]).astype(BF16).T
    zb_ref[...] = _dot(hm, wb_ref[...]).astype(BF16)
    zf_ref[...] = _dot(hm, wf_ref[...])
    gt_ref[...] = (_dot(hm, wg_ref[...]) + bg_ref[...]).T[0:gt_ref.shape[0], :]


def _project(R, mods, g, wqk, wv, wb, wf, wg, bg, wconv, l, n_x_tiles):
    B, T, D = R.shape
    nt = T // TOKEN_TILE
    hb = TOKEN_TILE // HALO
    n_halo_blocks = T // HALO
    who = lambda b, t: jnp.where(t >= n_x_tiles, B, b)
    nqk, nv, nb, nf, ng = wqk.shape[2], wv.shape[2], wb.shape[2], wf.shape[2], wg.shape[2]
    nq = nqk // 2
    n_gates = 4 * M_HEADS
    return pl.pallas_call(
        functools.partial(_proj_kernel, n_x_tiles=n_x_tiles, n_tiles=nt),
        grid=(B, nt),
        in_specs=[
            pl.BlockSpec((None, TOKEN_TILE, D), lambda b, t: (b, t, 0)),
            pl.BlockSpec((None, HALO, D), lambda b, t: (b, jnp.maximum(t * hb - 1, 0), 0)),
            pl.BlockSpec((None, HALO, D), lambda b, t: (b, jnp.minimum((t + 1) * hb, n_halo_blocks - 1), 0)),
            pl.BlockSpec((None, None, None, 3, D), lambda b, t: (l, who(b, t), 1, 0, 0)),
            pl.BlockSpec((None, 1, D), lambda b, t: (l, 0, 0)),
            _resident((None, D, nqk), lambda b, t: (l, 0, 0)),
            _resident((None, D, nv), lambda b, t: (l, 0, 0)),
            _resident((None, D, nb), lambda b, t: (l, 0, 0)),
            _resident((None, D, nf), lambda b, t: (l, 0, 0)),
            _resident((None, D, ng), lambda b, t: (l, 0, 0)),
            pl.BlockSpec((None, 1, ng), lambda b, t: (l, 0, 0)),
            pl.BlockSpec((None, M_CONV, nqk), lambda b, t: (l, 0, 0)),
        ],
        out_specs=[
            pl.BlockSpec((None, nq, TOKEN_TILE), lambda b, t: (b, 0, t)),
            pl.BlockSpec((None, TOKEN_TILE, nq), lambda b, t: (b, t, 0)),
            pl.BlockSpec((None, nv, TOKEN_TILE), lambda b, t: (b, 0, t)),
            pl.BlockSpec((None, TOKEN_TILE, nb), lambda b, t: (b, t, 0)),
            pl.BlockSpec((None, TOKEN_TILE, nf), lambda b, t: (b, t, 0)),
            pl.BlockSpec((None, n_gates, TOKEN_TILE), lambda b, t: (b, 0, t)),
        ],
        out_shape=[
            jax.ShapeDtypeStruct((B, nq, T), BF16),
            jax.ShapeDtypeStruct((B, T, nq), BF16),
            jax.ShapeDtypeStruct((B, nv, T), BF16),
            jax.ShapeDtypeStruct((B, T, nb), BF16),
            jax.ShapeDtypeStruct((B, T, nf), F32),
            jax.ShapeDtypeStruct((B, n_gates, T), F32),
        ],
        scratch_shapes=[pltpu.VMEM((TOKEN_TILE + 2 * HALO, D), BF16)],
        compiler_params=_params(2),
        name="in_projection",
    )(R, R, R, mods, g, wqk, wv, wb, wf, wg, bg, wconv)


def _mlstm_kernel(qt_ref, k_ref, vt_ref, gr_ref, o_ref, c_ref, pre_ref, *, n_x, seq):
    L = TOKEN_TILE
    row = lax.broadcasted_iota(jnp.int32, (L, L), 0)
    col = lax.broadcasted_iota(jnp.int32, (L, L), 1)
    lower = col <= row
    upper = col >= row
    diag = col == row
    ones_rows = jnp.ones((MLSTM_AUG, L), BF16)
    log2e = math.log2(math.e)

    def step(d, c0, m, accumulate):
        feeds = upper if d == 0 else lower
        sl = pl.ds(c0, L)
        qt, k = qt_ref[:, sl], k_ref[sl, :]
        vt = jnp.concatenate([vt_ref[:, sl], ones_rows], axis=0)
        b_row = pre_ref[2 * d, 0:1, sl]
        g_row = pre_ref[2 * d + 1, 0:1, sl]
        b_end = jnp.min(b_row, axis=1, keepdims=True)
        g_col = jnp.sum(jnp.where(diag, g_row, 0.0), axis=1, keepdims=True)
        g_masked = jnp.where(feeds, g_col, -jnp.inf)
        inter = b_row + m
        mj = jnp.maximum(inter, b_row + jnp.max(g_masked, axis=0, keepdims=True))
        st = _dot(k, qt) * jnp.exp2(g_masked + (b_row - mj))
        e_inter = jnp.exp2(inter - mj)
        c_old = c_ref[d]
        num = _dot(vt, st.astype(BF16)) + e_inter * _dot(c_old.astype(BF16), qt)
        den = num[M_HEAD_DIM:M_HEAD_DIM + 1]
        ht = num[:M_HEAD_DIM] * (1.0 / jnp.maximum(jnp.abs(den), jnp.exp2(-mj)))
        if accumulate:
            o_ref[:, sl] += ht
        else:
            o_ref[:, sl] = ht
        m_new = jnp.maximum(b_end + m, b_end + jnp.max(g_row, axis=1, keepdims=True))
        w_end = jnp.exp2(b_end + g_row - m_new)
        c_ref[d] = jnp.exp2(b_end + m - m_new) * c_old + _dot((vt.astype(F32) * w_end).astype(BF16), k)
        return m_new

    for c in range(pre_ref.shape[2] // L):
        cs = slice(c * L, (c + 1) * L)
        gr = gr_ref[:, cs]
        for d in range(2):
            lf8 = jnp.broadcast_to(jax.nn.log_sigmoid(gr[2 * d + 1:2 * d + 2]) * log2e, (8, L))
            lf_hi = lf8.astype(BF16)
            lf_lo = (lf8 - lf_hi.astype(F32)).astype(BF16)
            feeds_bf = jnp.where(upper if d == 0 else lower, 1.0, 0.0).astype(BF16)
            b8 = _dot(lf_hi, feeds_bf) + _dot(lf_lo, feeds_bf)
            pre_ref[2 * d, :, cs] = b8
            pre_ref[2 * d + 1, :, cs] = gr[2 * d:2 * d + 1] * log2e - b8

    c_ref[...] = jnp.zeros(c_ref.shape, F32)
    m0 = jnp.zeros((1, 1), F32)
    mf = step(0, seq, m0, False)
    mb = step(1, seq, m0, True)

    def both(accumulate):
        def body(i, carry):
            mf, mb = carry
            for u in range(MLSTM_UNROLL):
                ci = i * MLSTM_UNROLL + u
                mf = step(0, pl.multiple_of(ci * L, L), mf, accumulate)
                mb = step(1, pl.multiple_of((n_x - 1 - ci) * L, L), mb, accumulate)
            return mf, mb
        return body

    trips = n_x // MLSTM_UNROLL
    carry = lax.fori_loop(0, trips // 2, both(False), (mf, mb))
    lax.fori_loop(trips // 2, trips, both(True), carry)


def _mlstm(QT, K, VT, grow, seq):
    B, T, _ = K.shape
    dh = M_HEAD_DIM
    n_x = seq // TOKEN_TILE
    return pl.pallas_call(
        functools.partial(_mlstm_kernel, n_x=n_x, seq=seq),
        grid=(B, M_HEADS),
        in_specs=[
            pl.BlockSpec((None, dh, T), lambda b, h: (b, h, 0)),
            pl.BlockSpec((None, T, dh), lambda b, h: (b, 0, h)),
            pl.BlockSpec((None, dh, T), lambda b, h: (b, h, 0)),
            pl.BlockSpec((None, None, 4, T), lambda b, h: (b, h, 0, 0)),
        ],
        out_specs=pl.BlockSpec((None, dh, T), lambda b, h: (b, h, 0)),
        out_shape=jax.ShapeDtypeStruct((B, M_HEADS * dh, T), F32),
        scratch_shapes=[pltpu.VMEM((2, dh + MLSTM_AUG, dh), F32), pltpu.VMEM((4, 8, T), F32)],
        compiler_params=_params(2),
        name="mlstm_scan",
    )(QT, K, VT, grow)


def _mla_prep_kernel(zf_ref, cos_ref, sa_ref, sb_ref, gq_ref, gkv_ref, wuq_ref, wuk_ref, wuv_ref,
                     aq_ref, ak_ref, av_ref, *, scale):
    nb, rows, width = zf_ref.shape
    zf = zf_ref[...].reshape(nb * rows, width)
    cqn = _rms(zf[:, :A_QRANK], gq_ref[...]).astype(BF16)
    ckvn = _rms(zf[:, A_QRANK:ZF_KR], gkv_ref[...]).astype(BF16)
    cos, sa, sb = cos_ref[...], sa_ref[...], sb_ref[...]

    def rope(x):
        return x * cos + pltpu.roll(x, A_ROPE // 2, 1) * sa + pltpu.roll(x, LANE - A_ROPE // 2, 1) * sb

    qf = _dot(cqn, wuq_ref[...])
    kn = _dot(ckvn, wuk_ref[...])
    av = _dot(ckvn, wuv_ref[...]).astype(BF16)
    for i in range(nb):
        rs = slice(i * rows, (i + 1) * rows)
        for h in range(A_HEADS):
            o = h * A_QK
            aq_ref[i, :, o:o + A_NOPE] = (qf[rs, o:o + A_NOPE] * scale).astype(BF16)
            aq_ref[i, :, o + A_NOPE:o + A_QK] = (rope(qf[rs, o + A_NOPE:o + A_QK]) * scale).astype(BF16)
        kr = rope(zf[rs, ZF_KR:ZF_KR + LANE])
        for h in range(A_HEADS):
            kh = jnp.concatenate([kn[rs, h * A_NOPE:(h + 1) * A_NOPE], kr], axis=1)
            ak_ref[i, h * A_QK:(h + 1) * A_QK, :] = kh.T.astype(BF16)
        av_ref[i] = av[rs]


def _mla_prep(ZF, cos_t, sin_a, sin_b, gq, gkv, wuq, wuk, wuv, l):
    B, T, _ = ZF.shape
    nt = T // TOKEN_TILE
    nbt = FFN_BATCH_TILE
    scale = (A_NOPE + A_ROPE) ** -0.5 * math.log2(math.e)
    return pl.pallas_call(
        functools.partial(_mla_prep_kernel, scale=scale),
        grid=(B // nbt, nt),
        in_specs=[
            pl.BlockSpec((nbt, TOKEN_TILE, ZF_WIDTH), lambda b, t: (b, t, 0)),
            pl.BlockSpec((TOKEN_TILE, LANE), lambda b, t: (t, 0)),
            pl.BlockSpec((TOKEN_TILE, LANE), lambda b, t: (t, 0)),
            pl.BlockSpec((TOKEN_TILE, LANE), lambda b, t: (t, 0)),
            pl.BlockSpec((None, 1, A_QRANK), lambda b, t: (l, 0, 0)),
            pl.BlockSpec((None, 1, A_KVRANK), lambda b, t: (l, 0, 0)),
            _resident((None, A_QRANK, A_HEADS * A_QK), lambda b, t: (l, 0, 0)),
            _resident((None, A_KVRANK, A_HEADS * A_NOPE), lambda b, t: (l, 0, 0)),
            _resident((None, A_KVRANK, A_HEADS * A_VDIM), lambda b, t: (l, 0, 0)),
        ],
        out_specs=[
            pl.BlockSpec((nbt, TOKEN_TILE, A_HEADS * A_QK), lambda b, t: (b, t, 0)),
            pl.BlockSpec((nbt, A_HEADS * A_QK, TOKEN_TILE), lambda b, t: (b, 0, t)),
            pl.BlockSpec((nbt, TOKEN_TILE, A_HEADS * A_VDIM), lambda b, t: (b, t, 0)),
        ],
        out_shape=[
            jax.ShapeDtypeStruct((B, T, A_HEADS * A_QK), BF16),
            jax.ShapeDtypeStruct((B, A_HEADS * A_QK, T), BF16),
            jax.ShapeDtypeStruct((B, T, A_HEADS * A_VDIM), BF16),
        ],
        compiler_params=_params(2),
        name="mla_up_projection",
    )(ZF, cos_t, sin_a, sin_b, gq, gkv, wuq, wuk, wuv)


def _attn_kernel(q_ref, kt_ref, v_ref, o_ref, vaug_ref, *, n_split):
    nk = v_ref.shape[0]

    @pl.when(pl.program_id(2) == 0)
    def _():
        vaug_ref[:, :A_VDIM] = v_ref[...]
        lane = lax.broadcasted_iota(jnp.int32, (nk, LANE), 1)
        vaug_ref[:, A_VDIM:] = jnp.where(lane == 0, 1.0, 0.0).astype(BF16)

    rows = q_ref.shape[0] // n_split
    scores = [_dot(q_ref[r * rows:(r + 1) * rows, :], kt_ref[...]) for r in range(n_split)]
    for r, s in enumerate(scores):
        rs = slice(r * rows, (r + 1) * rows)
        p = jnp.exp2(s - jnp.max(s, axis=1, keepdims=True)).astype(BF16)
        acc = _dot(p, vaug_ref[...])
        o_ref[rs, :] = (acc[:, :A_VDIM] * (1.0 / acc[:, A_VDIM:A_VDIM + 1])).astype(o_ref.dtype)


def _attention(AQ, AK, AV, seq, with_ctx):
    B, T, _ = AQ.shape
    ctx_len = T - seq
    tq = ATTN_Q_TILE
    width = A_HEADS * A_VDIM
    a_lat = pl.pallas_call(
        functools.partial(_attn_kernel, n_split=2),
        grid=(B, A_HEADS, seq // tq),
        in_specs=[
            pl.BlockSpec((None, tq, A_QK), lambda b, h, t: (b, t, h)),
            pl.BlockSpec((None, A_QK, T), lambda b, h, t: (b, h, 0)),
            pl.BlockSpec((None, T, A_VDIM), lambda b, h, t: (b, 0, h)),
        ],
        out_specs=pl.BlockSpec((None, tq, A_VDIM), lambda b, h, t: (b, t, h)),
        out_shape=jax.ShapeDtypeStruct((B, seq, width), BF16),
        scratch_shapes=[pltpu.VMEM((T, A_VDIM + LANE), BF16)],
        compiler_params=_params(3),
        name="mla_attention_latent",
    )(AQ, AK, AV)
    if not with_ctx:
        return a_lat, None
    cb = seq // ctx_len
    a_ctx = pl.pallas_call(
        functools.partial(_attn_kernel, n_split=1),
        grid=(B, A_HEADS, 1),
        in_specs=[
            pl.BlockSpec((None, ctx_len, A_QK), lambda b, h, t: (b, cb, h)),
            pl.BlockSpec((None, A_QK, ctx_len), lambda b, h, t: (b, h, cb)),
            pl.BlockSpec((None, ctx_len, A_VDIM), lambda b, h, t: (b, cb, h)),
        ],
        out_specs=pl.BlockSpec((None, ctx_len, A_VDIM), lambda b, h, t: (b, 0, h)),
        out_shape=jax.ShapeDtypeStruct((B, ctx_len, width), BF16),
        scratch_shapes=[pltpu.VMEM((ctx_len, A_VDIM + LANE), BF16)],
        compiler_params=_params(3),
        name="mla_attention_context",
    )(AQ, AK, AV)
    return a_lat, a_ctx


def _merge_kernel(ht_ref, mo_ref, bg_ref, al_ref, ac_ref, r_ref, mod_ref, gmh_ref, wbm_ref, wba_ref, wout_ref, o_ref,
                  *, n_x_tiles):
    nb, rows, dm = r_ref.shape
    a = jnp.where(pl.program_id(1) < n_x_tiles, al_ref[...], ac_ref[...]).reshape(nb * rows, al_ref.shape[2])
    dh = M_HEAD_DIM

    def head_norm_t(b, i):
        x = ht_ref[b, i * dh:(i + 1) * dh, :]
        return x * lax.rsqrt(jnp.mean(x * x, axis=0, keepdims=True) + EPS)

    hn = jnp.concatenate(
        [jnp.concatenate([head_norm_t(b, i) for i in range(M_HEADS)], axis=0).T for b in range(nb)], axis=0) * gmh_ref[...]
    width = hn.shape[1]
    hm = (jax.nn.sigmoid(mo_ref[...].reshape(nb * rows, width).astype(F32)) * hn).astype(BF16)
    bg = bg_ref[...].reshape(nb * rows, 2 * width).astype(F32)
    u = jax.nn.sigmoid(bg[:, :width]) * _dot(hm, wbm_ref[...]) + jax.nn.sigmoid(bg[:, width:]) * _dot(a, wba_ref[...])
    y = _dot(u.astype(BF16), wout_ref[...])
    o_ref[...] = r_ref[...] + mod_ref[...][:, 2:3] * y.reshape(nb, rows, dm)


def _merge(HT, ZB, A_lat, A_ctx, R, mods, gmh, wbm, wba, wout, l, n_x_tiles, latent_only):
    B, T, D = R.shape
    nt = n_x_tiles if latent_only else T // TOKEN_TILE
    W = HT.shape[1]
    AW = A_lat.shape[2]
    assert W == D
    nbt = FFN_BATCH_TILE
    who = lambda b, t: jnp.where(t >= n_x_tiles, B // nbt, b)
    return pl.pallas_call(
        functools.partial(_merge_kernel, n_x_tiles=n_x_tiles),
        grid=(B // nbt, nt),
        in_specs=[
            pl.BlockSpec((nbt, W, TOKEN_TILE), lambda b, t: (b, 0, t)),
            pl.BlockSpec((nbt, TOKEN_TILE, W), lambda b, t: (b, t, 2)),
            pl.BlockSpec((nbt, TOKEN_TILE, 2 * D), lambda b, t: (b, t, 0)),
            pl.BlockSpec((nbt, TOKEN_TILE, AW), lambda b, t: (b, jnp.minimum(t, n_x_tiles - 1), 0)),
            pl.BlockSpec((nbt, TOKEN_TILE, AW), lambda b, t: (b, 0, 0)),
            pl.BlockSpec((nbt, TOKEN_TILE, D), lambda b, t: (b, t, 0)),
            pl.BlockSpec((None, nbt, None, 3, D), lambda b, t: (l, who(b, t), 1, 0, 0)),
            pl.BlockSpec((None, 1, W), lambda b, t: (l, 0, 0)),
            _resident((None, W, D), lambda b, t: (l, 0, 0)),
            _resident((None, AW, D), lambda b, t: (l, 0, 0)),
            _resident((None, D, D), lambda b, t: (l, 0, 0)),
        ],
        out_specs=pl.BlockSpec((nbt, TOKEN_TILE, D), lambda b, t: (b, t, 0)),
        out_shape=jax.ShapeDtypeStruct((B, nt * TOKEN_TILE, D), F32),
        compiler_params=_params(2),
        name="mixer_merge",
    )(HT, ZB, ZB, A_lat, A_ctx, R, mods, gmh, wbm, wba, wout)


def _rope_tables(seq, ctx_len):
    half = A_ROPE // 2
    axis_rot = A_ROPE // 2
    pos = jnp.arange(seq)
    rowp = (pos // GRID_W).astype(F32)
    colp = (pos % GRID_W).astype(F32)
    inv = ROPE_THETA ** (-jnp.arange(0, axis_rot, 2, dtype=F32) / axis_rot)
    ang = jnp.concatenate([rowp[:, None] * inv, colp[:, None] * inv], axis=-1)
    cos, sin = jnp.cos(ang), jnp.sin(ang)
    zeros = jnp.zeros((seq, half), F32)
    pad = jnp.zeros((seq, LANE - A_ROPE), F32)
    cos_x = jnp.concatenate([cos, cos, pad], axis=1)
    sa_x = jnp.concatenate([zeros, sin, pad], axis=1)
    sb_x = jnp.concatenate([-sin, zeros, pad], axis=1)
    cos_c = jnp.concatenate([jnp.ones((ctx_len, A_ROPE), F32), jnp.zeros((ctx_len, LANE - A_ROPE), F32)], axis=1)
    zc = jnp.zeros((ctx_len, LANE), F32)
    return (jnp.concatenate([cos_x, cos_c], axis=0), jnp.concatenate([sa_x, zc], axis=0),
            jnp.concatenate([sb_x, zc], axis=0))


def kernel(x, c, ctx, c_ctx, w_ada, b_ada, g_n1, g_n2, g_n3, w_ff1_up, w_ff1_dn, w_ff2_up, w_ff2_dn, w_in, b_gate, w_conv, g_mh, g_qa, g_kva, w_uq, w_ukv, w_bm, w_ba, w_out, g_final):
    B, S, D = x.shape
    C = ctx.shape[1]
    L = w_ada.shape[0]
    T = S + C
    MW = M_HEADS * M_HEAD_DIM
    assert C == TOKEN_TILE and S % (2 * MLSTM_UNROLL * TOKEN_TILE) == 0 and S % ATTN_Q_TILE == 0 and S % GRID_W == 0
    assert B % FFN_BATCH_TILE == 0 and B + FFN_BATCH_TILE <= 8 and D == MW
    n_x_tiles = S // TOKEN_TILE

    perm = jnp.concatenate([jnp.arange(0, A_ROPE, 2), jnp.arange(1, A_ROPE, 2)])
    o_gate = 4 * MW
    o_cq = o_gate + 4 * M_HEADS
    o_ckv = o_cq + A_QRANK
    o_kr = o_ckv + A_KVRANK
    o_br = o_kr + A_ROPE
    n_gates = 4 * M_HEADS
    gate_perm = jnp.array([dk * M_HEADS + h for h in range(M_HEADS) for dk in range(4)])
    wqk = w_in[:, :, :2 * MW].astype(BF16)
    wv = w_in[:, :, 2 * MW:3 * MW].astype(BF16)
    wb = jnp.concatenate([w_in[:, :, o_br:], w_in[:, :, 3 * MW:4 * MW]], axis=2).astype(BF16)
    wf = jnp.concatenate([
        w_in[:, :, o_cq:o_kr], w_in[:, :, o_kr:o_br][:, :, perm], jnp.zeros((L, D, LANE - A_ROPE), F32)],
        axis=2).astype(BF16)
    wg = jnp.concatenate([w_in[:, :, o_gate:o_cq][:, :, gate_perm], jnp.zeros((L, D, LANE - n_gates), F32)],
                         axis=2).astype(BF16)
    bg = jnp.concatenate([b_gate[:, gate_perm], jnp.zeros((L, LANE - n_gates), F32)], axis=1).reshape(L, 1, LANE)
    dq = A_NOPE + A_ROPE
    wuq4 = w_uq.reshape(L, A_QRANK, A_HEADS, dq)
    wuq = jnp.concatenate([wuq4[..., :A_NOPE], wuq4[..., A_NOPE:][..., perm],
                           jnp.zeros((L, A_QRANK, A_HEADS, A_QK - dq), F32)], axis=-1)
    wuq = wuq.reshape(L, A_QRANK, A_HEADS * A_QK).astype(BF16)
    wukv4 = w_ukv.reshape(L, A_KVRANK, A_HEADS, A_NOPE + A_VDIM)
    wuk = wukv4[..., :A_NOPE].reshape(L, A_KVRANK, A_HEADS * A_NOPE).astype(BF16)
    wuv = wukv4[..., A_NOPE:].reshape(L, A_KVRANK, A_HEADS * A_VDIM).astype(BF16)
    wup1, wdn1 = w_ff1_up.astype(BF16), w_ff1_dn.astype(BF16)
    wup2, wdn2 = w_ff2_up.astype(BF16), w_ff2_dn.astype(BF16)
    wbm, wba, wout = w_bm.astype(BF16), w_ba.astype(BF16), w_out.astype(BF16)
    row3 = lambda a: a.reshape(L, 1, a.shape[1])
    gn1, gn2, gn3, gmh, gqa, gkva = map(row3, (g_n1, g_n2, g_n3, g_mh, g_qa, g_kva))
    cos_t, sin_a, sin_b = _rope_tables(S, C)

    cvec = jnp.concatenate([c] + [c_ctx[None, :]] * FFN_BATCH_TILE + [jnp.zeros((8 - B - FFN_BATCH_TILE, D), F32)], axis=0)
    mods = _modulations(cvec, w_ada, b_ada).reshape(L, 8, N_MOD // 3, 3, D)

    R = (x, ctx)
    for l in range(L):
        R = _ffn(R, mods, gn1, wup1, wdn1, l, 0, n_x_tiles)
        QT, K, VT, ZB, ZF, GT = _project(R, mods, gn2, wqk, wv, wb, wf, wg, bg, w_conv, l, n_x_tiles)
        HT = _mlstm(QT, K, VT, GT.reshape(B, M_HEADS, 4, T), S)
        AQ, AK, AV = _mla_prep(ZF, cos_t, sin_a, sin_b, gqa, gkva, wuq, wuk, wuv, l)
        last = l == L - 1
        A_lat, A_ctx = _attention(AQ, AK, AV, S, with_ctx=not last)
        R = _merge(HT, ZB, A_lat, A_lat if last else A_ctx, R, mods, gmh, wbm, wba, wout, l, n_x_tiles, latent_only=last)
        R = _ffn(R, mods, gn3, wup2, wdn2, l, 2, n_x_tiles, final_g=g_final if last else None)
    return R
```

```python
import functools
import math

import jax
import jax.numpy as jnp
from jax import lax
from jax.experimental import pallas as pl
from jax.experimental.pallas import tpu as pltpu

F32 = jnp.float32
BF16 = jnp.bfloat16
EPS = 1e-6

GRID_W = 64
ROPE_THETA = 10000.0
M_HEADS = 4
M_HEAD_DIM = 256
M_CONV = 3
A_HEADS = 8
A_NOPE = 128
A_ROPE = 64
A_VDIM = 128
A_QRANK = 384
A_KVRANK = 256
N_MOD = 9

TOKEN_TILE = 256
HALO = 16
MLSTM_AUG = 16
FFN_BATCH_TILE = 2
MLSTM_UNROLL = 4
ATTN_Q_TILE = 1024
LANE = 128
A_QK = 2 * LANE
ZF_WIDTH = A_QRANK + A_KVRANK + LANE
ZF_KR = A_QRANK + A_KVRANK
VMEM_LIMIT = 48 * 1024 * 1024


def _dot(a, b):
    return jnp.dot(a, b, preferred_element_type=F32)


def _rms(x, g):
    return x * lax.rsqrt(jnp.mean(x * x, axis=-1, keepdims=True) + EPS) * g


def _norm_mod(x, g, shift, scale):
    return _rms(x, g) * (1 + scale) + shift


def _params(n_axes, vmem=VMEM_LIMIT):
    return pltpu.CompilerParams(dimension_semantics=("arbitrary",) * n_axes, vmem_limit_bytes=vmem)


def _resident(block, index_map):
    return pl.BlockSpec(block, index_map, pipeline_mode=pl.Buffered(1))


def _mod_kernel(c_ref, w_ref, b_ref, o_ref):
    cv = c_ref[...]
    s = (cv * jax.nn.sigmoid(cv)).astype(BF16)
    o_ref[...] = _dot(s, w_ref[...].astype(BF16)) + b_ref[...]


def _modulations(cvec, w_ada, b_ada):
    L, D, N = w_ada.shape
    tn = 1024
    return pl.pallas_call(
        _mod_kernel,
        grid=(L, N // tn),
        in_specs=[
            pl.BlockSpec((8, D), lambda l, j: (0, 0)),
            pl.BlockSpec((None, D, tn), lambda l, j: (l, 0, j)),
            pl.BlockSpec((None, 1, tn), lambda l, j: (l, 0, j)),
        ],
        out_specs=pl.BlockSpec((None, 8, tn), lambda l, j: (l, 0, j)),
        out_shape=jax.ShapeDtypeStruct((L, 8, N), F32),
        compiler_params=_params(2),
        name="adaln_mod",
    )(cvec, w_ada, b_ada.reshape(L, 1, N))


def _ffn_kernel(*refs, d_ff, n_chunks, n_x_tiles, split_input, final_norm):
    refs = list(refs)
    o_ref = refs.pop()
    gf_ref = refs.pop() if final_norm else None
    if split_input:
        x = jnp.where(pl.program_id(1) < n_x_tiles, refs[0][...], refs[1][...])
        refs = refs[2:]
    else:
        x = refs[0][...]
        refs = refs[1:]
    mod_ref, g_ref, wup_ref, wdn_ref = refs
    mod = mod_ref[...]
    nb, rows, dm = x.shape
    h = _norm_mod(x, g_ref[...], mod[:, 0:1], mod[:, 1:2]).astype(BF16).reshape(nb * rows, dm)
    tf = d_ff // n_chunks
    acc = None
    for ci in range(n_chunks):
        a = _dot(h, wup_ref[:, ci * tf:(ci + 1) * tf])
        b = _dot(h, wup_ref[:, d_ff + ci * tf:d_ff + (ci + 1) * tf])
        act = (a * jax.nn.sigmoid(a) * b).astype(BF16)
        part = _dot(act, wdn_ref[ci * tf:(ci + 1) * tf, :])
        acc = part if acc is None else acc + part
    y = x + (0.5 * mod[:, 2:3]) * acc.reshape(nb, rows, dm)
    o_ref[...] = _rms(y, gf_ref[...]) if final_norm else y


def _ffn(src, mods, g, w_up, w_dn, l, sub, n_x_tiles, final_g=None):
    split_input = isinstance(src, tuple)
    nbt = FFN_BATCH_TILE
    if split_input:
        xs, cs = src
        B, S, D = xs.shape
        T = S + cs.shape[1]
        row_specs = [
            pl.BlockSpec((nbt, TOKEN_TILE, D), lambda b, t: (b, jnp.minimum(t, n_x_tiles - 1), 0)),
            pl.BlockSpec((nbt, TOKEN_TILE, D), lambda b, t: (b, 0, 0)),
        ]
        rows = [xs, cs]
    else:
        B, T, D = src.shape
        row_specs = [pl.BlockSpec((nbt, TOKEN_TILE, D), lambda b, t: (b, t, 0))]
        rows = [src]
    d_ff = w_dn.shape[1]
    final_norm = final_g is not None
    nt = n_x_tiles if final_norm else T // TOKEN_TILE
    who = lambda b, t: jnp.where(t >= n_x_tiles, B // nbt, b)
    extra_specs = [pl.BlockSpec((1, D), lambda b, t: (0, 0))] if final_norm else []
    extra = [final_g.reshape(1, D)] if final_norm else []
    return pl.pallas_call(
        functools.partial(_ffn_kernel, d_ff=d_ff, n_chunks=1, n_x_tiles=n_x_tiles, split_input=split_input,
                          final_norm=final_norm),
        grid=(B // nbt, nt),
        in_specs=row_specs + [
            pl.BlockSpec((None, nbt, None, 3, D), lambda b, t: (l, who(b, t), sub, 0, 0)),
            pl.BlockSpec((None, 1, D), lambda b, t: (l, 0, 0)),
            _resident((None, D, 2 * d_ff), lambda b, t: (l, 0, 0)),
            _resident((None, d_ff, D), lambda b, t: (l, 0, 0)),
        ] + extra_specs,
        out_specs=pl.BlockSpec((nbt, TOKEN_TILE, D), lambda b, t: (b, t, 0)),
        out_shape=jax.ShapeDtypeStruct((B, nt * TOKEN_TILE, D), F32),
        compiler_params=_params(2),
        name="ffn_half_step",
    )(*rows, mods, g, w_up, w_dn, *extra)


def _proj_kernel(r_ref, rp_ref, rn_ref, mod_ref, g_ref, wqk_ref, wv_ref, wb_ref, wf_ref, wg_ref, bg_ref, wc_ref,
                 qt_ref, k_ref, vt_ref, zb_ref, zf_ref, gt_ref, hfull_ref, *, n_x_tiles, n_tiles):
    mod = mod_ref[...]
    g = g_ref[...]
    nm = lambda x: _norm_mod(x, g, mod[:, 0:1], mod[:, 1:2])
    nb, tile, dm = r_ref.shape
    t = pl.program_id(1)
    has_prev = jnp.logical_and(t != 0, t != n_x_tiles)
    has_next = jnp.logical_and(t != n_x_tiles - 1, t != n_tiles - 1)
    hm3 = nm(r_ref[...]).astype(BF16)
    hp3 = jnp.where(has_prev, nm(rp_ref[...]), 0.0).astype(BF16)
    hn3 = jnp.where(has_next, nm(rn_ref[...]), 0.0).astype(BF16)
    span = tile + 2 * HALO
    for i in range(nb):
        hfull_ref[i * span:i * span + HALO, :] = hp3[i]
        hfull_ref[i * span + HALO:i * span + HALO + tile, :] = hm3[i]
        hfull_ref[i * span + HALO + tile:(i + 1) * span, :] = hn3[i]
    hfull = hfull_ref[...]
    hm = hm3.reshape(nb * tile, dm)
    rows = nb * span
    width = wqk_ref.shape[1]
    cw = 512
    for ci in range(width // cw):
        cs = slice(ci * cw, (ci + 1) * cw)
        z = _dot(hfull, wqk_ref[:, cs])
        w = wc_ref[:, cs]
        cv = w[0:1] * pltpu.roll(z, 1, 0) + w[1:2] * z + w[2:3] * pltpu.roll(z, rows - 1, 0)
        y = cv * jax.nn.sigmoid(cv)
        for i in range(nb):
            yi = y[i * span + HALO:i * span + HALO + tile]
            if ci * cw < width // 2:
                qt_ref[i, cs, :] = (yi * (M_HEAD_DIM ** -0.5)).astype(BF16).T
            else:
                k_ref[i, :, ci * cw - width // 2:(ci + 1) * cw - width // 2] = yi.astype(BF16)
    v = _dot(hm, wv_ref[...]).astype(BF16)
    zg = _dot(hm, wg_ref[...]) + bg_ref[...]
    for i in range(nb):
        vt_ref[i] = v[i * tile:(i + 1) * tile].T
        gt_ref[i] = zg[i * tile:(i + 1) * tile].T[0:gt_ref.shape[1], :]
    zb_ref[...] = _dot(hm, wb_ref[...]).astype(BF16).reshape(zb_ref.shape)
    zf_ref[...] = _dot(hm, wf_ref[...]).reshape(zf_ref.shape)


def _project(R, mods, g, wqk, wv, wb, wf, wg, bg, wconv, l, n_x_tiles):
    B, T, D = R.shape
    nt = T // TOKEN_TILE
    hb = TOKEN_TILE // HALO
    n_halo_blocks = T // HALO
    nbt = FFN_BATCH_TILE
    who = lambda b, t: jnp.where(t >= n_x_tiles, B // nbt, b)
    nqk, nv, nb, nf, ng = wqk.shape[2], wv.shape[2], wb.shape[2], wf.shape[2], wg.shape[2]
    nq = nqk // 2
    n_gates = 4 * M_HEADS
    return pl.pallas_call(
        functools.partial(_proj_kernel, n_x_tiles=n_x_tiles, n_tiles=nt),
        grid=(B // nbt, nt),
        in_specs=[
            pl.BlockSpec((nbt, TOKEN_TILE, D), lambda b, t: (b, t, 0)),
            pl.BlockSpec((nbt, HALO, D), lambda b, t: (b, jnp.maximum(t * hb - 1, 0), 0)),
            pl.BlockSpec((nbt, HALO, D), lambda b, t: (b, jnp.minimum((t + 1) * hb, n_halo_blocks - 1), 0)),
            pl.BlockSpec((None, nbt, None, 3, D), lambda b, t: (l, who(b, t), 1, 0, 0)),
            pl.BlockSpec((None, 1, D), lambda b, t: (l, 0, 0)),
            _resident((None, D, nqk), lambda b, t: (l, 0, 0)),
            _resident((None, D, nv), lambda b, t: (l, 0, 0)),
            _resident((None, D, nb), lambda b, t: (l, 0, 0)),
            _resident((None, D, nf), lambda b, t: (l, 0, 0)),
            _resident((None, D, ng), lambda b, t: (l, 0, 0)),
            pl.BlockSpec((None, 1, ng), lambda b, t: (l, 0, 0)),
            pl.BlockSpec((None, M_CONV, nqk), lambda b, t: (l, 0, 0)),
        ],
        out_specs=[
            pl.BlockSpec((nbt, nq, TOKEN_TILE), lambda b, t: (b, 0, t)),
            pl.BlockSpec((nbt, TOKEN_TILE, nq), lambda b, t: (b, t, 0)),
            pl.BlockSpec((nbt, nv, TOKEN_TILE), lambda b, t: (b, 0, t)),
            pl.BlockSpec((nbt, TOKEN_TILE, nb), lambda b, t: (b, t, 0)),
            pl.BlockSpec((nbt, TOKEN_TILE, nf), lambda b, t: (b, t, 0)),
            pl.BlockSpec((nbt, n_gates, TOKEN_TILE), lambda b, t: (b, 0, t)),
        ],
        out_shape=[
            jax.ShapeDtypeStruct((B, nq, T), BF16),
            jax.ShapeDtypeStruct((B, T, nq), BF16),
            jax.ShapeDtypeStruct((B, nv, T), BF16),
            jax.ShapeDtypeStruct((B, T, nb), BF16),
            jax.ShapeDtypeStruct((B, T, nf), F32),
            jax.ShapeDtypeStruct((B, n_gates, T), F32),
        ],
        scratch_shapes=[pltpu.VMEM((nbt * (TOKEN_TILE + 2 * HALO), D), BF16)],
        compiler_params=_params(2),
        name="in_projection",
    )(R, R, R, mods, g, wqk, wv, wb, wf, wg, bg, wconv)


def _mlstm_kernel(qt_ref, k_ref, vt_ref, gr_ref, o_ref, c_ref, pre_ref, *, n_x, seq):
    L = TOKEN_TILE
    row = lax.broadcasted_iota(jnp.int32, (L, L), 0)
    col = lax.broadcasted_iota(jnp.int32, (L, L), 1)
    lower = col <= row
    upper = col >= row
    diag = col == row
    ones_rows = jnp.ones((MLSTM_AUG, L), BF16)
    log2e = math.log2(math.e)

    def step(d, c0, m, accumulate):
        feeds = upper if d == 0 else lower
        sl = pl.ds(c0, L)
        qt, k = qt_ref[:, sl], k_ref[sl, :]
        vt = jnp.concatenate([vt_ref[:, sl], ones_rows], axis=0)
        b_row = pre_ref[2 * d, 0:1, sl]
        g_row = pre_ref[2 * d + 1, 0:1, sl]
        b_end = jnp.min(b_row, axis=1, keepdims=True)
        g_col = jnp.sum(jnp.where(diag, g_row, 0.0), axis=1, keepdims=True)
        g_masked = jnp.where(feeds, g_col, -jnp.inf)
        inter = b_row + m
        mj = jnp.maximum(inter, b_row + jnp.max(g_masked, axis=0, keepdims=True))
        st = _dot(k, qt) * jnp.exp2(g_masked + (b_row - mj))
        e_inter = jnp.exp2(inter - mj)
        c_old = c_ref[d]
        num = _dot(vt, st.astype(BF16)) + e_inter * _dot(c_old.astype(BF16), qt)
        den = num[M_HEAD_DIM:M_HEAD_DIM + 1]
        ht = num[:M_HEAD_DIM] * (1.0 / jnp.maximum(jnp.abs(den), jnp.exp2(-mj)))
        if accumulate:
            o_ref[:, sl] += ht
        else:
            o_ref[:, sl] = ht
        m_new = jnp.maximum(b_end + m, b_end + jnp.max(g_row, axis=1, keepdims=True))
        w_end = jnp.exp2(b_end + g_row - m_new)
        c_ref[d] = jnp.exp2(b_end + m - m_new) * c_old + _dot((vt.astype(F32) * w_end).astype(BF16), k)
        return m_new

    for c in range(pre_ref.shape[2] // L):
        cs = slice(c * L, (c + 1) * L)
        gr = gr_ref[:, cs]
        for d in range(2):
            lf8 = jnp.broadcast_to(jax.nn.log_sigmoid(gr[2 * d + 1:2 * d + 2]) * log2e, (8, L))
            lf_hi = lf8.astype(BF16)
            lf_lo = (lf8 - lf_hi.astype(F32)).astype(BF16)
            feeds_bf = jnp.where(upper if d == 0 else lower, 1.0, 0.0).astype(BF16)
            b8 = _dot(lf_hi, feeds_bf) + _dot(lf_lo, feeds_bf)
            pre_ref[2 * d, :, cs] = b8
            pre_ref[2 * d + 1, :, cs] = gr[2 * d:2 * d + 1] * log2e - b8

    c_ref[...] = jnp.zeros(c_ref.shape, F32)
    m0 = jnp.zeros((1, 1), F32)
    mf = step(0, seq, m0, False)
    mb = step(1, seq, m0, True)

    def both(accumulate):
        def body(i, carry):
            mf, mb = carry
            for u in range(MLSTM_UNROLL):
                ci = i * MLSTM_UNROLL + u
                mf = step(0, pl.multiple_of(ci * L, L), mf, accumulate)
                mb = step(1, pl.multiple_of((n_x - 1 - ci) * L, L), mb, accumulate)
            return mf, mb
        return body

    trips = n_x // MLSTM_UNROLL
    carry = lax.fori_loop(0, trips // 2, both(False), (mf, mb))
    lax.fori_loop(trips // 2, trips, both(True), carry)


def _mlstm(QT, K, VT, grow, seq):
    B, T, _ = K.shape
    dh = M_HEAD_DIM
    n_x = seq // TOKEN_TILE
    return pl.pallas_call(
        functools.partial(_mlstm_kernel, n_x=n_x, seq=seq),
        grid=(B, M_HEADS),
        in_specs=[
            pl.BlockSpec((None, dh, T), lambda b, h: (b, h, 0)),
            pl.BlockSpec((None, T, dh), lambda b, h: (b, 0, h)),
            pl.BlockSpec((None, dh, T), lambda b, h: (b, h, 0)),
            pl.BlockSpec((None, None, 4, T), lambda b, h: (b, h, 0, 0)),
        ],
        out_specs=pl.BlockSpec((None, dh, T), lambda b, h: (b, h, 0)),
        out_shape=jax.ShapeDtypeStruct((B, M_HEADS * dh, T), F32),
        scratch_shapes=[pltpu.VMEM((2, dh + MLSTM_AUG, dh), F32), pltpu.VMEM((4, 8, T), F32)],
        compiler_params=_params(2),
        name="mlstm_scan",
    )(QT, K, VT, grow)


def _mla_prep_kernel(zf_ref, cos_ref, sa_ref, sb_ref, gq_ref, gkv_ref, wuq_ref, wuk_ref, wuv_ref,
                     aq_ref, ak_ref, av_ref, *, scale):
    nb, rows, width = zf_ref.shape
    zf = zf_ref[...].reshape(nb * rows, width)
    cqn = _rms(zf[:, :A_QRANK], gq_ref[...]).astype(BF16)
    ckvn = _rms(zf[:, A_QRANK:ZF_KR], gkv_ref[...]).astype(BF16)
    cos, sa, sb = cos_ref[...], sa_ref[...], sb_ref[...]

    def rope(x):
        return x * cos + pltpu.roll(x, A_ROPE // 2, 1) * sa + pltpu.roll(x, LANE - A_ROPE // 2, 1) * sb

    qf = _dot(cqn, wuq_ref[...])
    kn = _dot(ckvn, wuk_ref[...])
    av = _dot(ckvn, wuv_ref[...]).astype(BF16)
    for i in range(nb):
        rs = slice(i * rows, (i + 1) * rows)
        for h in range(A_HEADS):
            o = h * A_QK
            aq_ref[i, :, o:o + A_NOPE] = (qf[rs, o:o + A_NOPE] * scale).astype(BF16)
            aq_ref[i, :, o + A_NOPE:o + A_QK] = (rope(qf[rs, o + A_NOPE:o + A_QK]) * scale).astype(BF16)
        kr = rope(zf[rs, ZF_KR:ZF_KR + LANE])
        for h in range(A_HEADS):
            kh = jnp.concatenate([kn[rs, h * A_NOPE:(h + 1) * A_NOPE], kr], axis=1)
            ak_ref[i, h * A_QK:(h + 1) * A_QK, :] = kh.T.astype(BF16)
        av_ref[i] = av[rs]


def _mla_prep(ZF, cos_t, sin_a, sin_b, gq, gkv, wuq, wuk, wuv, l):
    B, T, _ = ZF.shape
    nt = T // TOKEN_TILE
    nbt = FFN_BATCH_TILE
    scale = (A_NOPE + A_ROPE) ** -0.5 * math.log2(math.e)
    return pl.pallas_call(
        functools.partial(_mla_prep_kernel, scale=scale),
        grid=(B // nbt, nt),
        in_specs=[
            pl.BlockSpec((nbt, TOKEN_TILE, ZF_WIDTH), lambda b, t: (b, t, 0)),
            pl.BlockSpec((TOKEN_TILE, LANE), lambda b, t: (t, 0)),
            pl.BlockSpec((TOKEN_TILE, LANE), lambda b, t: (t, 0)),
            pl.BlockSpec((TOKEN_TILE, LANE), lambda b, t: (t, 0)),
            pl.BlockSpec((None, 1, A_QRANK), lambda b, t: (l, 0, 0)),
            pl.BlockSpec((None, 1, A_KVRANK), lambda b, t: (l, 0, 0)),
            _resident((None, A_QRANK, A_HEADS * A_QK), lambda b, t: (l, 0, 0)),
            _resident((None, A_KVRANK, A_HEADS * A_NOPE), lambda b, t: (l, 0, 0)),
            _resident((None, A_KVRANK, A_HEADS * A_VDIM), lambda b, t: (l, 0, 0)),
        ],
        out_specs=[
            pl.BlockSpec((nbt, TOKEN_TILE, A_HEADS * A_QK), lambda b, t: (b, t, 0)),
            pl.BlockSpec((nbt, A_HEADS * A_QK, TOKEN_TILE), lambda b, t: (b, 0, t)),
            pl.BlockSpec((nbt, TOKEN_TILE, A_HEADS * A_VDIM), lambda b, t: (b, t, 0)),
        ],
        out_shape=[
            jax.ShapeDtypeStruct((B, T, A_HEADS * A_QK), BF16),
            jax.ShapeDtypeStruct((B, A_HEADS * A_QK, T), BF16),
            jax.ShapeDtypeStruct((B, T, A_HEADS * A_VDIM), BF16),
        ],
        compiler_params=_params(2),
        name="mla_up_projection",
    )(ZF, cos_t, sin_a, sin_b, gq, gkv, wuq, wuk, wuv)


def _attn_kernel(q_ref, kt_ref, v_ref, o_ref, vaug_ref, *, n_split):
    nk = v_ref.shape[0]

    @pl.when(pl.program_id(2) == 0)
    def _():
        vaug_ref[:, :A_VDIM] = v_ref[...]
        lane = lax.broadcasted_iota(jnp.int32, (nk, LANE), 1)
        vaug_ref[:, A_VDIM:] = jnp.where(lane == 0, 1.0, 0.0).astype(BF16)

    rows = q_ref.shape[0] // n_split
    scores = [_dot(q_ref[r * rows:(r + 1) * rows, :], kt_ref[...]) for r in range(n_split)]
    for r, s in enumerate(scores):
        rs = slice(r * rows, (r + 1) * rows)
        p = jnp.exp2(s - jnp.max(s, axis=1, keepdims=True)).astype(BF16)
        acc = _dot(p, vaug_ref[...])
        o_ref[rs, :] = (acc[:, :A_VDIM] * (1.0 / acc[:, A_VDIM:A_VDIM + 1])).astype(o_ref.dtype)


def _attention(AQ, AK, AV, seq, with_ctx):
    B, T, _ = AQ.shape
    ctx_len = T - seq
    tq = ATTN_Q_TILE
    width = A_HEADS * A_VDIM
    a_lat = pl.pallas_call(
        functools.partial(_attn_kernel, n_split=2),
        grid=(B, A_HEADS, seq // tq),
        in_specs=[
            pl.BlockSpec((None, tq, A_QK), lambda b, h, t: (b, t, h)),
            pl.BlockSpec((None, A_QK, T), lambda b, h, t: (b, h, 0)),
            pl.BlockSpec((None, T, A_VDIM), lambda b, h, t: (b, 0, h)),
        ],
        out_specs=pl.BlockSpec((None, tq, A_VDIM), lambda b, h, t: (b, t, h)),
        out_shape=jax.ShapeDtypeStruct((B, seq, width), BF16),
        scratch_shapes=[pltpu.VMEM((T, A_VDIM + LANE), BF16)],
        compiler_params=_params(3),
        name="mla_attention_latent",
    )(AQ, AK, AV)
    if not with_ctx:
        return a_lat, None
    cb = seq // ctx_len
    a_ctx = pl.pallas_call(
        functools.partial(_attn_kernel, n_split=1),
        grid=(B, A_HEADS, 1),
        in_specs=[
            pl.BlockSpec((None, ctx_len, A_QK), lambda b, h, t: (b, cb, h)),
            pl.BlockSpec((None, A_QK, ctx_len), lambda b, h, t: (b, h, cb)),
            pl.BlockSpec((None, ctx_len, A_VDIM), lambda b, h, t: (b, cb, h)),
        ],
        out_specs=pl.BlockSpec((None, ctx_len, A_VDIM), lambda b, h, t: (b, 0, h)),
        out_shape=jax.ShapeDtypeStruct((B, ctx_len, width), BF16),
        scratch_shapes=[pltpu.VMEM((ctx_len, A_VDIM + LANE), BF16)],
        compiler_params=_params(3),
        name="mla_attention_context",
    )(AQ, AK, AV)
    return a_lat, a_ctx


def _merge_kernel(ht_ref, mo_ref, bg_ref, al_ref, ac_ref, r_ref, mod_ref, gmh_ref, wbm_ref, wba_ref, wout_ref, o_ref,
                  *, n_x_tiles):
    nb, rows, dm = r_ref.shape
    a = jnp.where(pl.program_id(1) < n_x_tiles, al_ref[...], ac_ref[...]).reshape(nb * rows, al_ref.shape[2])
    dh = M_HEAD_DIM

    def head_norm_t(b, i):
        x = ht_ref[b, i * dh:(i + 1) * dh, :]
        return x * lax.rsqrt(jnp.mean(x * x, axis=0, keepdims=True) + EPS)

    hn = jnp.concatenate(
        [jnp.concatenate([head_norm_t(b, i) for i in range(M_HEADS)], axis=0).T for b in range(nb)], axis=0) * gmh_ref[...]
    width = hn.shape[1]
    hm = (jax.nn.sigmoid(mo_ref[...].reshape(nb * rows, width).astype(F32)) * hn).astype(BF16)
    bg = bg_ref[...].reshape(nb * rows, 2 * width).astype(F32)
    u = jax.nn.sigmoid(bg[:, :width]) * _dot(hm, wbm_ref[...]) + jax.nn.sigmoid(bg[:, width:]) * _dot(a, wba_ref[...])
    y = _dot(u.astype(BF16), wout_ref[...])
    o_ref[...] = r_ref[...] + mod_ref[...][:, 2:3] * y.reshape(nb, rows, dm)


def _merge(HT, ZB, A_lat, A_ctx, R, mods, gmh, wbm, wba, wout, l, n_x_tiles, latent_only):
    B, T, D = R.shape
    nt = n_x_tiles if latent_only else T // TOKEN_TILE
    W = HT.shape[1]
    AW = A_lat.shape[2]
    assert W == D
    nbt = FFN_BATCH_TILE
    who = lambda b, t: jnp.where(t >= n_x_tiles, B // nbt, b)
    return pl.pallas_call(
        functools.partial(_merge_kernel, n_x_tiles=n_x_tiles),
        grid=(B // nbt, nt),
        in_specs=[
            pl.BlockSpec((nbt, W, TOKEN_TILE), lambda b, t: (b, 0, t)),
            pl.BlockSpec((nbt, TOKEN_TILE, W), lambda b, t: (b, t, 2)),
            pl.BlockSpec((nbt, TOKEN_TILE, 2 * D), lambda b, t: (b, t, 0)),
            pl.BlockSpec((nbt, TOKEN_TILE, AW), lambda b, t: (b, jnp.minimum(t, n_x_tiles - 1), 0)),
            pl.BlockSpec((nbt, TOKEN_TILE, AW), lambda b, t: (b, 0, 0)),
            pl.BlockSpec((nbt, TOKEN_TILE, D), lambda b, t: (b, t, 0)),
            pl.BlockSpec((None, nbt, None, 3, D), lambda b, t: (l, who(b, t), 1, 0, 0)),
            pl.BlockSpec((None, 1, W), lambda b, t: (l, 0, 0)),
            _resident((None, W, D), lambda b, t: (l, 0, 0)),
            _resident((None, AW, D), lambda b, t: (l, 0, 0)),
            _resident((None, D, D), lambda b, t: (l, 0, 0)),
        ],
        out_specs=pl.BlockSpec((nbt, TOKEN_TILE, D), lambda b, t: (b, t, 0)),
        out_shape=jax.ShapeDtypeStruct((B, nt * TOKEN_TILE, D), F32),
        compiler_params=_params(2),
        name="mixer_merge",
    )(HT, ZB, ZB, A_lat, A_ctx, R, mods, gmh, wbm, wba, wout)


def _rope_tables(seq, ctx_len):
    half = A_ROPE // 2
    axis_rot = A_ROPE // 2
    pos = jnp.arange(seq)
    rowp = (pos // GRID_W).astype(F32)
    colp = (pos % GRID_W).astype(F32)
    inv = ROPE_THETA ** (-jnp.arange(0, axis_rot, 2, dtype=F32) / axis_rot)
    ang = jnp.concatenate([rowp[:, None] * inv, colp[:, None] * inv], axis=-1)
    cos, sin = jnp.cos(ang), jnp.sin(ang)
    zeros = jnp.zeros((seq, half), F32)
    pad = jnp.zeros((seq, LANE - A_ROPE), F32)
    cos_x = jnp.concatenate([cos, cos, pad], axis=1)
    sa_x = jnp.concatenate([zeros, sin, pad], axis=1)
    sb_x = jnp.concatenate([-sin, zeros, pad], axis=1)
    cos_c = jnp.concatenate([jnp.ones((ctx_len, A_ROPE), F32), jnp.zeros((ctx_len, LANE - A_ROPE), F32)], axis=1)
    zc = jnp.zeros((ctx_len, LANE), F32)
    return (jnp.concatenate([cos_x, cos_c], axis=0), jnp.concatenate([sa_x, zc], axis=0),
            jnp.concatenate([sb_x, zc], axis=0))


def kernel(x, c, ctx, c_ctx, w_ada, b_ada, g_n1, g_n2, g_n3, w_ff1_up, w_ff1_dn, w_ff2_up, w_ff2_dn, w_in, b_gate, w_conv, g_mh, g_qa, g_kva, w_uq, w_ukv, w_bm, w_ba, w_out, g_final):
    B, S, D = x.shape
    C = ctx.shape[1]
    L = w_ada.shape[0]
    T = S + C
    MW = M_HEADS * M_HEAD_DIM
    assert C == TOKEN_TILE and S % (2 * MLSTM_UNROLL * TOKEN_TILE) == 0 and S % ATTN_Q_TILE == 0 and S % GRID_W == 0
    assert B % FFN_BATCH_TILE == 0 and B + FFN_BATCH_TILE <= 8 and D == MW
    n_x_tiles = S // TOKEN_TILE

    perm = jnp.concatenate([jnp.arange(0, A_ROPE, 2), jnp.arange(1, A_ROPE, 2)])
    o_gate = 4 * MW
    o_cq = o_gate + 4 * M_HEADS
    o_ckv = o_cq + A_QRANK
    o_kr = o_ckv + A_KVRANK
    o_br = o_kr + A_ROPE
    n_gates = 4 * M_HEADS
    gate_perm = jnp.array([dk * M_HEADS + h for h in range(M_HEADS) for dk in range(4)])
    wqk = w_in[:, :, :2 * MW].astype(BF16)
    wv = w_in[:, :, 2 * MW:3 * MW].astype(BF16)
    wb = jnp.concatenate([w_in[:, :, o_br:], w_in[:, :, 3 * MW:4 * MW]], axis=2).astype(BF16)
    wf = jnp.concatenate([
        w_in[:, :, o_cq:o_kr], w_in[:, :, o_kr:o_br][:, :, perm], jnp.zeros((L, D, LANE - A_ROPE), F32)],
        axis=2).astype(BF16)
    wg = jnp.concatenate([w_in[:, :, o_gate:o_cq][:, :, gate_perm], jnp.zeros((L, D, LANE - n_gates), F32)],
                         axis=2).astype(BF16)
    bg = jnp.concatenate([b_gate[:, gate_perm], jnp.zeros((L, LANE - n_gates), F32)], axis=1).reshape(L, 1, LANE)
    dq = A_NOPE + A_ROPE
    wuq4 = w_uq.reshape(L, A_QRANK, A_HEADS, dq)
    wuq = jnp.concatenate([wuq4[..., :A_NOPE], wuq4[..., A_NOPE:][..., perm],
                           jnp.zeros((L, A_QRANK, A_HEADS, A_QK - dq), F32)], axis=-1)
    wuq = wuq.reshape(L, A_QRANK, A_HEADS * A_QK).astype(BF16)
    wukv4 = w_ukv.reshape(L, A_KVRANK, A_HEADS, A_NOPE + A_VDIM)
    wuk = wukv4[..., :A_NOPE].reshape(L, A_KVRANK, A_HEADS * A_NOPE).astype(BF16)
    wuv = wukv4[..., A_NOPE:].reshape(L, A_KVRANK, A_HEADS * A_VDIM).astype(BF16)
    wup1, wdn1 = w_ff1_up.astype(BF16), w_ff1_dn.astype(BF16)
    wup2, wdn2 = w_ff2_up.astype(BF16), w_ff2_dn.astype(BF16)
    wbm, wba, wout = w_bm.astype(BF16), w_ba.astype(BF16), w_out.astype(BF16)
    row3 = lambda a: a.reshape(L, 1, a.shape[1])
    gn1, gn2, gn3, gmh, gqa, gkva = map(row3, (g_n1, g_n2, g_n3, g_mh, g_qa, g_kva))
    cos_t, sin_a, sin_b = _rope_tables(S, C)

    cvec = jnp.concatenate([c] + [c_ctx[None, :]] * FFN_BATCH_TILE + [jnp.zeros((8 - B - FFN_BATCH_TILE, D), F32)], axis=0)
    mods = _modulations(cvec, w_ada, b_ada).reshape(L, 8, N_MOD // 3, 3, D)

    R = (x, ctx)
    for l in range(L):
        R = _ffn(R, mods, gn1, wup1, wdn1, l, 0, n_x_tiles)
        QT, K, VT, ZB, ZF, GT = _project(R, mods, gn2, wqk, wv, wb, wf, wg, bg, w_conv, l, n_x_tiles)
        HT = _mlstm(QT, K, VT, GT.reshape(B, M_HEADS, 4, T), S)
        AQ, AK, AV = _mla_prep(ZF, cos_t, sin_a, sin_b, gqa, gkva, wuq, wuk, wuv, l)
        last = l == L - 1
        A_lat, A_ctx = _attention(AQ, AK, AV, S, with_ctx=not last)
        R = _merge(HT, ZB, A_lat, A_lat if last else A_ctx, R, mods, gmh, wbm, wba, wout, l, n_x_tiles, latent_only=last)
        R = _ffn(R, mods, gn3, wup2, wdn2, l, 2, n_x_tiles, final_g=g_final if last else None)
    return R
```

```python
import functools
import math

import jax
import jax.numpy as jnp
from jax import lax
from jax.experimental import pallas as pl
from jax.experimental.pallas import tpu as pltpu

F32 = jnp.float32
BF16 = jnp.bfloat16
EPS = 1e-6

GRID_W = 64
ROPE_THETA = 10000.0
M_HEADS = 4
M_HEAD_DIM = 256
M_CONV = 3
A_HEADS = 8
A_NOPE = 128
A_ROPE = 64
A_VDIM = 128
A_QRANK = 384
A_KVRANK = 256
N_MOD = 9

TOKEN_TILE = 256
HALO = 16
MLSTM_AUG = 16
FFN_BATCH_TILE = 2
MLSTM_UNROLL = 4
ATTN_Q_TILE = 1024
LANE = 128
A_QK = 2 * LANE
ZF_WIDTH = A_QRANK + A_KVRANK + LANE
ZF_KR = A_QRANK + A_KVRANK
VMEM_LIMIT = 48 * 1024 * 1024


def _dot(a, b):
    return jnp.dot(a, b, preferred_element_type=F32)


def _rms(x, g):
    return x * lax.rsqrt(jnp.mean(x * x, axis=-1, keepdims=True) + EPS) * g


def _norm_mod(x, g, shift, scale):
    return _rms(x, g) * (1 + scale) + shift


def _params(n_axes, vmem=VMEM_LIMIT):
    return pltpu.CompilerParams(dimension_semantics=("arbitrary",) * n_axes, vmem_limit_bytes=vmem)


def _resident(block, index_map):
    return pl.BlockSpec(block, index_map, pipeline_mode=pl.Buffered(1))


def _mod_kernel(c_ref, w_ref, b_ref, o_ref):
    cv = c_ref[...]
    s = (cv * jax.nn.sigmoid(cv)).astype(BF16)
    o_ref[...] = _dot(s, w_ref[...].astype(BF16)) + b_ref[...]


def _modulations(cvec, w_ada, b_ada):
    L, D, N = w_ada.shape
    tn = 2304
    assert N % tn == 0 and tn % LANE == 0
    return pl.pallas_call(
        _mod_kernel,
        grid=(L, N // tn),
        in_specs=[
            pl.BlockSpec((8, D), lambda l, j: (0, 0)),
            pl.BlockSpec((None, D, tn), lambda l, j: (l, 0, j)),
            pl.BlockSpec((None, 1, tn), lambda l, j: (l, 0, j)),
        ],
        out_specs=pl.BlockSpec((None, 8, tn), lambda l, j: (l, 0, j)),
        out_shape=jax.ShapeDtypeStruct((L, 8, N), F32),
        compiler_params=_params(2),
        name="adaln_mod",
    )(cvec, w_ada, b_ada.reshape(L, 1, N))


def _ffn_kernel(*refs, d_ff, n_chunks, n_x_tiles, split_input, final_norm):
    refs = list(refs)
    o_ref = refs.pop()
    gf_ref = refs.pop() if final_norm else None
    if split_input:
        x = jnp.where(pl.program_id(1) < n_x_tiles, refs[0][...], refs[1][...])
        refs = refs[2:]
    else:
        x = refs[0][...]
        refs = refs[1:]
    mod_ref, g_ref, wup_ref, wdn_ref = refs
    mod = mod_ref[...]
    nb, rows, dm = x.shape
    h = _norm_mod(x, g_ref[...], mod[:, 0:1], mod[:, 1:2]).astype(BF16).reshape(nb * rows, dm)
    tf = d_ff // n_chunks
    acc = None
    for ci in range(n_chunks):
        a = _dot(h, wup_ref[:, ci * tf:(ci + 1) * tf])
        b = _dot(h, wup_ref[:, d_ff + ci * tf:d_ff + (ci + 1) * tf])
        act = (a * jax.nn.sigmoid(a) * b).astype(BF16)
        part = _dot(act, wdn_ref[ci * tf:(ci + 1) * tf, :])
        acc = part if acc is None else acc + part
    y = x + (0.5 * mod[:, 2:3]) * acc.reshape(nb, rows, dm)
    o_ref[...] = _rms(y, gf_ref[...]) if final_norm else y


def _ffn(src, mods, g, w_up, w_dn, l, sub, n_x_tiles, final_g=None):
    split_input = isinstance(src, tuple)
    nbt = FFN_BATCH_TILE
    if split_input:
        xs, cs = src
        B, S, D = xs.shape
        T = S + cs.shape[1]
        row_specs = [
            pl.BlockSpec((nbt, TOKEN_TILE, D), lambda b, t: (b, jnp.minimum(t, n_x_tiles - 1), 0)),
            pl.BlockSpec((nbt, TOKEN_TILE, D), lambda b, t: (b, 0, 0)),
        ]
        rows = [xs, cs]
    else:
        B, T, D = src.shape
        row_specs = [pl.BlockSpec((nbt, TOKEN_TILE, D), lambda b, t: (b, t, 0))]
        rows = [src]
    d_ff = w_dn.shape[1]
    final_norm = final_g is not None
    nt = n_x_tiles if final_norm else T // TOKEN_TILE
    who = lambda b, t: jnp.where(t >= n_x_tiles, B // nbt, b)
    extra_specs = [pl.BlockSpec((1, D), lambda b, t: (0, 0))] if final_norm else []
    extra = [final_g.reshape(1, D)] if final_norm else []
    return pl.pallas_call(
        functools.partial(_ffn_kernel, d_ff=d_ff, n_chunks=1, n_x_tiles=n_x_tiles, split_input=split_input,
                          final_norm=final_norm),
        grid=(B // nbt, nt),
        in_specs=row_specs + [
            pl.BlockSpec((None, nbt, None, 3, D), lambda b, t: (l, who(b, t), sub, 0, 0)),
            pl.BlockSpec((None, 1, D), lambda b, t: (l, 0, 0)),
            _resident((None, D, 2 * d_ff), lambda b, t: (l, 0, 0)),
            _resident((None, d_ff, D), lambda b, t: (l, 0, 0)),
        ] + extra_specs,
        out_specs=pl.BlockSpec((nbt, TOKEN_TILE, D), lambda b, t: (b, t, 0)),
        out_shape=jax.ShapeDtypeStruct((B, nt * TOKEN_TILE, D), F32),
        compiler_params=_params(2),
        name="ffn_half_step",
    )(*rows, mods, g, w_up, w_dn, *extra)


def _proj_kernel(r_ref, rp_ref, rn_ref, mod_ref, g_ref, wqk_ref, wv_ref, wb_ref, wf_ref, wg_ref, bg_ref, wc_ref,
                 qt_ref, k_ref, vt_ref, zb_ref, zf_ref, gt_ref, hfull_ref, *, n_x_tiles, n_tiles):
    mod = mod_ref[...]
    g = g_ref[...]
    nm = lambda x: _norm_mod(x, g, mod[:, 0:1], mod[:, 1:2])
    nb, tile, dm = r_ref.shape
    t = pl.program_id(1)
    has_prev = jnp.logical_and(t != 0, t != n_x_tiles)
    has_next = jnp.logical_and(t != n_x_tiles - 1, t != n_tiles - 1)
    hm3 = nm(r_ref[...]).astype(BF16)
    hp3 = jnp.where(has_prev, nm(rp_ref[...]), 0.0).astype(BF16)
    hn3 = jnp.where(has_next, nm(rn_ref[...]), 0.0).astype(BF16)
    span = tile + 2 * HALO
    for i in range(nb):
        hfull_ref[i * span:i * span + HALO, :] = hp3[i]
        hfull_ref[i * span + HALO:i * span + HALO + tile, :] = hm3[i]
        hfull_ref[i * span + HALO + tile:(i + 1) * span, :] = hn3[i]
    hfull = hfull_ref[...]
    hm = hm3.reshape(nb * tile, dm)
    rows = nb * span
    width = wqk_ref.shape[1]
    cw = 512
    for ci in range(width // cw):
        cs = slice(ci * cw, (ci + 1) * cw)
        z = _dot(hfull, wqk_ref[:, cs])
        w = wc_ref[:, cs]
        cv = w[0:1] * pltpu.roll(z, 1, 0) + w[1:2] * z + w[2:3] * pltpu.roll(z, rows - 1, 0)
        y = cv * jax.nn.sigmoid(cv)
        for i in range(nb):
            yi = y[i * span + HALO:i * span + HALO + tile]
            if ci * cw < width // 2:
                qt_ref[i, cs, :] = (yi * (M_HEAD_DIM ** -0.5)).astype(BF16).T
            else:
                k_ref[i, :, ci * cw - width // 2:(ci + 1) * cw - width // 2] = yi.astype(BF16)
    v = _dot(hm, wv_ref[...]).astype(BF16)
    zg = _dot(hm, wg_ref[...]) + bg_ref[...]
    for i in range(nb):
        vt_ref[i] = v[i * tile:(i + 1) * tile].T
        gt_ref[i] = zg[i * tile:(i + 1) * tile].T[0:gt_ref.shape[1], :]
    zb_ref[...] = _dot(hm, wb_ref[...]).astype(BF16).reshape(zb_ref.shape)
    zf_ref[...] = _dot(hm, wf_ref[...]).reshape(zf_ref.shape)


def _project(R, mods, g, wqk, wv, wb, wf, wg, bg, wconv, l, n_x_tiles):
    B, T, D = R.shape
    nt = T // TOKEN_TILE
    hb = TOKEN_TILE // HALO
    n_halo_blocks = T // HALO
    nbt = FFN_BATCH_TILE
    who = lambda b, t: jnp.where(t >= n_x_tiles, B // nbt, b)
    nqk, nv, nb, nf, ng = wqk.shape[2], wv.shape[2], wb.shape[2], wf.shape[2], wg.shape[2]
    nq = nqk // 2
    n_gates = 4 * M_HEADS
    return pl.pallas_call(
        functools.partial(_proj_kernel, n_x_tiles=n_x_tiles, n_tiles=nt),
        grid=(B // nbt, nt),
        in_specs=[
            pl.BlockSpec((nbt, TOKEN_TILE, D), lambda b, t: (b, t, 0)),
            pl.BlockSpec((nbt, HALO, D), lambda b, t: (b, jnp.maximum(t * hb - 1, 0), 0)),
            pl.BlockSpec((nbt, HALO, D), lambda b, t: (b, jnp.minimum((t + 1) * hb, n_halo_blocks - 1), 0)),
            pl.BlockSpec((None, nbt, None, 3, D), lambda b, t: (l, who(b, t), 1, 0, 0)),
            pl.BlockSpec((None, 1, D), lambda b, t: (l, 0, 0)),
            _resident((None, D, nqk), lambda b, t: (l, 0, 0)),
            _resident((None, D, nv), lambda b, t: (l, 0, 0)),
            _resident((None, D, nb), lambda b, t: (l, 0, 0)),
            _resident((None, D, nf), lambda b, t: (l, 0, 0)),
            _resident((None, D, ng), lambda b, t: (l, 0, 0)),
            pl.BlockSpec((None, 1, ng), lambda b, t: (l, 0, 0)),
            pl.BlockSpec((None, M_CONV, nqk), lambda b, t: (l, 0, 0)),
        ],
        out_specs=[
            pl.BlockSpec((nbt, nq, TOKEN_TILE), lambda b, t: (b, 0, t)),
            pl.BlockSpec((nbt, TOKEN_TILE, nq), lambda b, t: (b, t, 0)),
            pl.BlockSpec((nbt, nv, TOKEN_TILE), lambda b, t: (b, 0, t)),
            pl.BlockSpec((nbt, TOKEN_TILE, nb), lambda b, t: (b, t, 0)),
            pl.BlockSpec((nbt, TOKEN_TILE, nf), lambda b, t: (b, t, 0)),
            pl.BlockSpec((nbt, n_gates, TOKEN_TILE), lambda b, t: (b, 0, t)),
        ],
        out_shape=[
            jax.ShapeDtypeStruct((B, nq, T), BF16),
            jax.ShapeDtypeStruct((B, T, nq), BF16),
            jax.ShapeDtypeStruct((B, nv, T), BF16),
            jax.ShapeDtypeStruct((B, T, nb), BF16),
            jax.ShapeDtypeStruct((B, T, nf), F32),
            jax.ShapeDtypeStruct((B, n_gates, T), F32),
        ],
        scratch_shapes=[pltpu.VMEM((nbt * (TOKEN_TILE + 2 * HALO), D), BF16)],
        compiler_params=_params(2),
        name="in_projection",
    )(R, R, R, mods, g, wqk, wv, wb, wf, wg, bg, wconv)


def _mlstm_kernel(qt_ref, k_ref, vt_ref, gr_ref, o_ref, c_ref, pre_ref, *, n_x, seq):
    L = TOKEN_TILE
    row = lax.broadcasted_iota(jnp.int32, (L, L), 0)
    col = lax.broadcasted_iota(jnp.int32, (L, L), 1)
    lower = col <= row
    upper = col >= row
    diag = col == row
    ones_rows = jnp.ones((MLSTM_AUG, L), BF16)
    log2e = math.log2(math.e)

    def step(d, c0, m, accumulate):
        feeds = upper if d == 0 else lower
        sl = pl.ds(c0, L)
        qt, k = qt_ref[:, sl], k_ref[sl, :]
        vt = jnp.concatenate([vt_ref[:, sl], ones_rows], axis=0)
        b_row = pre_ref[2 * d, 0:1, sl]
        g_row = pre_ref[2 * d + 1, 0:1, sl]
        b_end = jnp.min(b_row, axis=1, keepdims=True)
        g_col = jnp.sum(jnp.where(diag, g_row, 0.0), axis=1, keepdims=True)
        g_masked = jnp.where(feeds, g_col, -jnp.inf)
        inter = b_row + m
        mj = jnp.maximum(inter, b_row + jnp.max(g_masked, axis=0, keepdims=True))
        st = _dot(k, qt) * jnp.exp2(g_masked + (b_row - mj))
        e_inter = jnp.exp2(inter - mj)
        c_old = c_ref[d]
        num = _dot(vt, st.astype(BF16)) + e_inter * _dot(c_old.astype(BF16), qt)
        den = num[M_HEAD_DIM:M_HEAD_DIM + 1]
        ht = num[:M_HEAD_DIM] * (1.0 / jnp.maximum(jnp.abs(den), jnp.exp2(-mj)))
        if accumulate:
            o_ref[:, sl] += ht
        else:
            o_ref[:, sl] = ht
        m_new = jnp.maximum(b_end + m, b_end + jnp.max(g_row, axis=1, keepdims=True))
        w_end = jnp.exp2(b_end + g_row - m_new)
        c_ref[d] = jnp.exp2(b_end + m - m_new) * c_old + _dot((vt.astype(F32) * w_end).astype(BF16), k)
        return m_new

    for c in range(pre_ref.shape[2] // L):
        cs = slice(c * L, (c + 1) * L)
        gr = gr_ref[:, cs]
        for d in range(2):
            lf8 = jnp.broadcast_to(jax.nn.log_sigmoid(gr[2 * d + 1:2 * d + 2]) * log2e, (8, L))
            lf_hi = lf8.astype(BF16)
            lf_lo = (lf8 - lf_hi.astype(F32)).astype(BF16)
            feeds_bf = jnp.where(upper if d == 0 else lower, 1.0, 0.0).astype(BF16)
            b8 = _dot(lf_hi, feeds_bf) + _dot(lf_lo, feeds_bf)
            pre_ref[2 * d, :, cs] = b8
            pre_ref[2 * d + 1, :, cs] = gr[2 * d:2 * d + 1] * log2e - b8

    c_ref[...] = jnp.zeros(c_ref.shape, F32)
    m0 = jnp.zeros((1, 1), F32)
    mf = step(0, seq, m0, False)
    mb = step(1, seq, m0, True)

    def both(accumulate):
        def body(i, carry):
            mf, mb = carry
            for u in range(MLSTM_UNROLL):
                ci = i * MLSTM_UNROLL + u
                mf = step(0, pl.multiple_of(ci * L, L), mf, accumulate)
                mb = step(1, pl.multiple_of((n_x - 1 - ci) * L, L), mb, accumulate)
            return mf, mb
        return body

    trips = n_x // MLSTM_UNROLL
    carry = lax.fori_loop(0, trips // 2, both(False), (mf, mb))
    lax.fori_loop(trips // 2, trips, both(True), carry)


def _mlstm(QT, K, VT, grow, seq):
    B, T, _ = K.shape
    dh = M_HEAD_DIM
    n_x = seq // TOKEN_TILE
    return pl.pallas_call(
        functools.partial(_mlstm_kernel, n_x=n_x, seq=seq),
        grid=(B, M_HEADS),
        in_specs=[
            pl.BlockSpec((None, dh, T), lambda b, h: (b, h, 0)),
            pl.BlockSpec((None, T, dh), lambda b, h: (b, 0, h)),
            pl.BlockSpec((None, dh, T), lambda b, h: (b, h, 0)),
            pl.BlockSpec((None, None, 4, T), lambda b, h: (b, h, 0, 0)),
        ],
        out_specs=pl.BlockSpec((None, dh, T), lambda b, h: (b, h, 0)),
        out_shape=jax.ShapeDtypeStruct((B, M_HEADS * dh, T), F32),
        scratch_shapes=[pltpu.VMEM((2, dh + MLSTM_AUG, dh), F32), pltpu.VMEM((4, 8, T), F32)],
        compiler_params=_params(2),
        name="mlstm_scan",
    )(QT, K, VT, grow)


def _mla_prep_kernel(zf_ref, cos_ref, sa_ref, sb_ref, gq_ref, gkv_ref, wuq_ref, wuk_ref, wuv_ref,
                     aq_ref, ak_ref, av_ref, *, scale):
    nb, rows, width = zf_ref.shape
    zf = zf_ref[...].reshape(nb * rows, width)
    cqn = _rms(zf[:, :A_QRANK], gq_ref[...]).astype(BF16)
    ckvn = _rms(zf[:, A_QRANK:ZF_KR], gkv_ref[...]).astype(BF16)
    cos, sa, sb = cos_ref[...], sa_ref[...], sb_ref[...]

    def rope(x):
        return x * cos + pltpu.roll(x, A_ROPE // 2, 1) * sa + pltpu.roll(x, LANE - A_ROPE // 2, 1) * sb

    qf = _dot(cqn, wuq_ref[...])
    kn = _dot(ckvn, wuk_ref[...])
    av = _dot(ckvn, wuv_ref[...]).astype(BF16)
    for i in range(nb):
        rs = slice(i * rows, (i + 1) * rows)
        for h in range(A_HEADS):
            o = h * A_QK
            aq_ref[i, :, o:o + A_NOPE] = (qf[rs, o:o + A_NOPE] * scale).astype(BF16)
            aq_ref[i, :, o + A_NOPE:o + A_QK] = (rope(qf[rs, o + A_NOPE:o + A_QK]) * scale).astype(BF16)
        kr = rope(zf[rs, ZF_KR:ZF_KR + LANE])
        for h in range(A_HEADS):
            kh = jnp.concatenate([kn[rs, h * A_NOPE:(h + 1) * A_NOPE], kr], axis=1)
            ak_ref[i, h * A_QK:(h + 1) * A_QK, :] = kh.T.astype(BF16)
        av_ref[i] = av[rs]


def _mla_prep(ZF, cos_t, sin_a, sin_b, gq, gkv, wuq, wuk, wuv, l):
    B, T, _ = ZF.shape
    nt = T // TOKEN_TILE
    nbt = FFN_BATCH_TILE
    scale = (A_NOPE + A_ROPE) ** -0.5 * math.log2(math.e)
    return pl.pallas_call(
        functools.partial(_mla_prep_kernel, scale=scale),
        grid=(B // nbt, nt),
        in_specs=[
            pl.BlockSpec((nbt, TOKEN_TILE, ZF_WIDTH), lambda b, t: (b, t, 0)),
            pl.BlockSpec((TOKEN_TILE, LANE), lambda b, t: (t, 0)),
            pl.BlockSpec((TOKEN_TILE, LANE), lambda b, t: (t, 0)),
            pl.BlockSpec((TOKEN_TILE, LANE), lambda b, t: (t, 0)),
            pl.BlockSpec((None, 1, A_QRANK), lambda b, t: (l, 0, 0)),
            pl.BlockSpec((None, 1, A_KVRANK), lambda b, t: (l, 0, 0)),
            _resident((None, A_QRANK, A_HEADS * A_QK), lambda b, t: (l, 0, 0)),
            _resident((None, A_KVRANK, A_HEADS * A_NOPE), lambda b, t: (l, 0, 0)),
            _resident((None, A_KVRANK, A_HEADS * A_VDIM), lambda b, t: (l, 0, 0)),
        ],
        out_specs=[
            pl.BlockSpec((nbt, TOKEN_TILE, A_HEADS * A_QK), lambda b, t: (b, t, 0)),
            pl.BlockSpec((nbt, A_HEADS * A_QK, TOKEN_TILE), lambda b, t: (b, 0, t)),
            pl.BlockSpec((nbt, TOKEN_TILE, A_HEADS * A_VDIM), lambda b, t: (b, t, 0)),
        ],
        out_shape=[
            jax.ShapeDtypeStruct((B, T, A_HEADS * A_QK), BF16),
            jax.ShapeDtypeStruct((B, A_HEADS * A_QK, T), BF16),
            jax.ShapeDtypeStruct((B, T, A_HEADS * A_VDIM), BF16),
        ],
        compiler_params=_params(2),
        name="mla_up_projection",
    )(ZF, cos_t, sin_a, sin_b, gq, gkv, wuq, wuk, wuv)


def _attn_kernel(q_ref, kt_ref, v_ref, o_ref, vaug_ref, *, n_split):
    nk = v_ref.shape[0]

    @pl.when(pl.program_id(2) == 0)
    def _():
        vaug_ref[:, :A_VDIM] = v_ref[...]
        lane = lax.broadcasted_iota(jnp.int32, (nk, LANE), 1)
        vaug_ref[:, A_VDIM:] = jnp.where(lane == 0, 1.0, 0.0).astype(BF16)

    rows = q_ref.shape[0] // n_split
    scores = [_dot(q_ref[r * rows:(r + 1) * rows, :], kt_ref[...]) for r in range(n_split)]
    for r, s in enumerate(scores):
        rs = slice(r * rows, (r + 1) * rows)
        p = jnp.exp2(s - jnp.max(s, axis=1, keepdims=True)).astype(BF16)
        acc = _dot(p, vaug_ref[...])
        o_ref[rs, :] = (acc[:, :A_VDIM] * (1.0 / acc[:, A_VDIM:A_VDIM + 1])).astype(o_ref.dtype)


def _ctx_attn_kernel(q_ref, kt_ref, v_ref, o_ref):
    for h in range(A_HEADS):
        s = _dot(q_ref[:, h * A_QK:(h + 1) * A_QK], kt_ref[h * A_QK:(h + 1) * A_QK, :])
        p = jnp.exp2(s - jnp.max(s, axis=1, keepdims=True))
        acc = _dot(p.astype(BF16), v_ref[:, h * A_VDIM:(h + 1) * A_VDIM])
        o_ref[:, h * A_VDIM:(h + 1) * A_VDIM] = (acc * (1.0 / jnp.sum(p, axis=1, keepdims=True))).astype(o_ref.dtype)


def _attention(AQ, AK, AV, seq, with_ctx):
    B, T, _ = AQ.shape
    ctx_len = T - seq
    tq = ATTN_Q_TILE
    width = A_HEADS * A_VDIM
    a_lat = pl.pallas_call(
        functools.partial(_attn_kernel, n_split=2),
        grid=(B, A_HEADS, seq // tq),
        in_specs=[
            pl.BlockSpec((None, tq, A_QK), lambda b, h, t: (b, t, h)),
            pl.BlockSpec((None, A_QK, T), lambda b, h, t: (b, h, 0)),
            pl.BlockSpec((None, T, A_VDIM), lambda b, h, t: (b, 0, h)),
        ],
        out_specs=pl.BlockSpec((None, tq, A_VDIM), lambda b, h, t: (b, t, h)),
        out_shape=jax.ShapeDtypeStruct((B, seq, width), BF16),
        scratch_shapes=[pltpu.VMEM((T, A_VDIM + LANE), BF16)],
        compiler_params=_params(3),
        name="mla_attention_latent",
    )(AQ, AK, AV)
    if not with_ctx:
        return a_lat, None
    cb = seq // ctx_len
    a_ctx = pl.pallas_call(
        _ctx_attn_kernel,
        grid=(B,),
        in_specs=[
            pl.BlockSpec((None, ctx_len, A_HEADS * A_QK), lambda b: (b, cb, 0)),
            pl.BlockSpec((None, A_HEADS * A_QK, ctx_len), lambda b: (b, 0, cb)),
            pl.BlockSpec((None, ctx_len, width), lambda b: (b, cb, 0)),
        ],
        out_specs=pl.BlockSpec((None, ctx_len, width), lambda b: (b, 0, 0)),
        out_shape=jax.ShapeDtypeStruct((B, ctx_len, width), BF16),
        compiler_params=_params(1),
        name="mla_attention_context",
    )(AQ, AK, AV)
    return a_lat, a_ctx


def _merge_kernel(ht_ref, mo_ref, bg_ref, al_ref, ac_ref, r_ref, mod_ref, gmh_ref, wbm_ref, wba_ref, wout_ref, o_ref,
                  *, n_x_tiles):
    nb, rows, dm = r_ref.shape
    a = jnp.where(pl.program_id(1) < n_x_tiles, al_ref[...], ac_ref[...]).reshape(nb * rows, al_ref.shape[2])
    dh = M_HEAD_DIM

    def head_norm_t(b, i):
        x = ht_ref[b, i * dh:(i + 1) * dh, :]
        return x * lax.rsqrt(jnp.mean(x * x, axis=0, keepdims=True) + EPS)

    hn = jnp.concatenate(
        [jnp.concatenate([head_norm_t(b, i) for i in range(M_HEADS)], axis=0).T for b in range(nb)], axis=0) * gmh_ref[...]
    width = hn.shape[1]
    hm = (jax.nn.sigmoid(mo_ref[...].reshape(nb * rows, width).astype(F32)) * hn).astype(BF16)
    bg = bg_ref[...].reshape(nb * rows, 2 * width).astype(F32)
    u = jax.nn.sigmoid(bg[:, :width]) * _dot(hm, wbm_ref[...]) + jax.nn.sigmoid(bg[:, width:]) * _dot(a, wba_ref[...])
    y = _dot(u.astype(BF16), wout_ref[...])
    o_ref[...] = r_ref[...] + mod_ref[...][:, 2:3] * y.reshape(nb, rows, dm)


def _merge(HT, ZB, A_lat, A_ctx, R, mods, gmh, wbm, wba, wout, l, n_x_tiles, latent_only):
    B, T, D = R.shape
    nt = n_x_tiles if latent_only else T // TOKEN_TILE
    W = HT.shape[1]
    AW = A_lat.shape[2]
    assert W == D
    nbt = FFN_BATCH_TILE
    who = lambda b, t: jnp.where(t >= n_x_tiles, B // nbt, b)
    return pl.pallas_call(
        functools.partial(_merge_kernel, n_x_tiles=n_x_tiles),
        grid=(B // nbt, nt),
        in_specs=[
            pl.BlockSpec((nbt, W, TOKEN_TILE), lambda b, t: (b, 0, t)),
            pl.BlockSpec((nbt, TOKEN_TILE, W), lambda b, t: (b, t, 2)),
            pl.BlockSpec((nbt, TOKEN_TILE, 2 * D), lambda b, t: (b, t, 0)),
            pl.BlockSpec((nbt, TOKEN_TILE, AW), lambda b, t: (b, jnp.minimum(t, n_x_tiles - 1), 0)),
            pl.BlockSpec((nbt, TOKEN_TILE, AW), lambda b, t: (b, 0, 0)),
            pl.BlockSpec((nbt, TOKEN_TILE, D), lambda b, t: (b, t, 0)),
            pl.BlockSpec((None, nbt, None, 3, D), lambda b, t: (l, who(b, t), 1, 0, 0)),
            pl.BlockSpec((None, 1, W), lambda b, t: (l, 0, 0)),
            _resident((None, W, D), lambda b, t: (l, 0, 0)),
            _resident((None, AW, D), lambda b, t: (l, 0, 0)),
            _resident((None, D, D), lambda b, t: (l, 0, 0)),
        ],
        out_specs=pl.BlockSpec((nbt, TOKEN_TILE, D), lambda b, t: (b, t, 0)),
        out_shape=jax.ShapeDtypeStruct((B, nt * TOKEN_TILE, D), F32),
        compiler_params=_params(2),
        name="mixer_merge",
    )(HT, ZB, ZB, A_lat, A_ctx, R, mods, gmh, wbm, wba, wout)


def _rope_tables(seq, ctx_len):
    half = A_ROPE // 2
    axis_rot = A_ROPE // 2
    pos = jnp.arange(seq)
    rowp = (pos // GRID_W).astype(F32)
    colp = (pos % GRID_W).astype(F32)
    inv = ROPE_THETA ** (-jnp.arange(0, axis_rot, 2, dtype=F32) / axis_rot)
    ang = jnp.concatenate([rowp[:, None] * inv, colp[:, None] * inv], axis=-1)
    cos, sin = jnp.cos(ang), jnp.sin(ang)
    zeros = jnp.zeros((seq, half), F32)
    pad = jnp.zeros((seq, LANE - A_ROPE), F32)
    cos_x = jnp.concatenate([cos, cos, pad], axis=1)
    sa_x = jnp.concatenate([zeros, sin, pad], axis=1)
    sb_x = jnp.concatenate([-sin, zeros, pad], axis=1)
    cos_c = jnp.concatenate([jnp.ones((ctx_len, A_ROPE), F32), jnp.zeros((ctx_len, LANE - A_ROPE), F32)], axis=1)
    zc = jnp.zeros((ctx_len, LANE), F32)
    return (jnp.concatenate([cos_x, cos_c], axis=0), jnp.concatenate([sa_x, zc], axis=0),
            jnp.concatenate([sb_x, zc], axis=0))


def kernel(x, c, ctx, c_ctx, w_ada, b_ada, g_n1, g_n2, g_n3, w_ff1_up, w_ff1_dn, w_ff2_up, w_ff2_dn, w_in, b_gate, w_conv, g_mh, g_qa, g_kva, w_uq, w_ukv, w_bm, w_ba, w_out, g_final):
    B, S, D = x.shape
    C = ctx.shape[1]
    L = w_ada.shape[0]
    T = S + C
    MW = M_HEADS * M_HEAD_DIM
    assert C == TOKEN_TILE and S % (2 * MLSTM_UNROLL * TOKEN_TILE) == 0 and S % ATTN_Q_TILE == 0 and S % GRID_W == 0
    assert B % FFN_BATCH_TILE == 0 and B + FFN_BATCH_TILE <= 8 and D == MW
    n_x_tiles = S // TOKEN_TILE

    perm = jnp.concatenate([jnp.arange(0, A_ROPE, 2), jnp.arange(1, A_ROPE, 2)])
    o_gate = 4 * MW
    o_cq = o_gate + 4 * M_HEADS
    o_ckv = o_cq + A_QRANK
    o_kr = o_ckv + A_KVRANK
    o_br = o_kr + A_ROPE
    n_gates = 4 * M_HEADS
    gate_perm = jnp.array([dk * M_HEADS + h for h in range(M_HEADS) for dk in range(4)])
    wqk = w_in[:, :, :2 * MW].astype(BF16)
    wv = w_in[:, :, 2 * MW:3 * MW].astype(BF16)
    wb = jnp.concatenate([w_in[:, :, o_br:], w_in[:, :, 3 * MW:4 * MW]], axis=2).astype(BF16)
    wf = jnp.concatenate([
        w_in[:, :, o_cq:o_kr], w_in[:, :, o_kr:o_br][:, :, perm], jnp.zeros((L, D, LANE - A_ROPE), F32)],
        axis=2).astype(BF16)
    wg = jnp.concatenate([w_in[:, :, o_gate:o_cq][:, :, gate_perm], jnp.zeros((L, D, LANE - n_gates), F32)],
                         axis=2).astype(BF16)
    bg = jnp.concatenate([b_gate[:, gate_perm], jnp.zeros((L, LANE - n_gates), F32)], axis=1).reshape(L, 1, LANE)
    dq = A_NOPE + A_ROPE
    wuq4 = w_uq.reshape(L, A_QRANK, A_HEADS, dq)
    wuq = jnp.concatenate([wuq4[..., :A_NOPE], wuq4[..., A_NOPE:][..., perm],
                           jnp.zeros((L, A_QRANK, A_HEADS, A_QK - dq), F32)], axis=-1)
    wuq = wuq.reshape(L, A_QRANK, A_HEADS * A_QK).astype(BF16)
    wukv4 = w_ukv.reshape(L, A_KVRANK, A_HEADS, A_NOPE + A_VDIM)
    wuk = wukv4[..., :A_NOPE].reshape(L, A_KVRANK, A_HEADS * A_NOPE).astype(BF16)
    wuv = wukv4[..., A_NOPE:].reshape(L, A_KVRANK, A_HEADS * A_VDIM).astype(BF16)
    wup1, wdn1 = w_ff1_up.astype(BF16), w_ff1_dn.astype(BF16)
    wup2, wdn2 = w_ff2_up.astype(BF16), w_ff2_dn.astype(BF16)
    wbm, wba, wout = w_bm.astype(BF16), w_ba.astype(BF16), w_out.astype(BF16)
    row3 = lambda a: a.reshape(L, 1, a.shape[1])
    gn1, gn2, gn3, gmh, gqa, gkva = map(row3, (g_n1, g_n2, g_n3, g_mh, g_qa, g_kva))
    cos_t, sin_a, sin_b = _rope_tables(S, C)

    cvec = jnp.concatenate([c] + [c_ctx[None, :]] * FFN_BATCH_TILE + [jnp.zeros((8 - B - FFN_BATCH_TILE, D), F32)], axis=0)
    mods = _modulations(cvec, w_ada, b_ada).reshape(L, 8, N_MOD // 3, 3, D)

    R = (x, ctx)
    for l in range(L):
        R = _ffn(R, mods, gn1, wup1, wdn1, l, 0, n_x_tiles)
        QT, K, VT, ZB, ZF, GT = _project(R, mods, gn2, wqk, wv, wb, wf, wg, bg, w_conv, l, n_x_tiles)
        HT = _mlstm(QT, K, VT, GT.reshape(B, M_HEADS, 4, T), S)
        AQ, AK, AV = _mla_prep(ZF, cos_t, sin_a, sin_b, gqa, gkva, wuq, wuk, wuv, l)
        last = l == L - 1
        A_lat, A_ctx = _attention(AQ, AK, AV, S, with_ctx=not last)
        R = _merge(HT, ZB, A_lat, A_lat if last else A_ctx, R, mods, gmh, wbm, wba, wout, l, n_x_tiles, latent_only=last)
        R = _ffn(R, mods, gn3, wup2, wdn2, l, 2, n_x_tiles, final_g=g_final if last else None)
    return R
```

```python
import functools
import math

import jax
import jax.numpy as jnp
from jax import lax
from jax.experimental import pallas as pl
from jax.experimental.pallas import tpu as pltpu

F32 = jnp.float32
BF16 = jnp.bfloat16
EPS = 1e-6

GRID_W = 64
ROPE_THETA = 10000.0
M_HEADS = 4
M_HEAD_DIM = 256
M_CONV = 3
A_HEADS = 8
A_NOPE = 128
A_ROPE = 64
A_VDIM = 128
A_QRANK = 384
A_KVRANK = 256
N_MOD = 9

TOKEN_TILE = 256
HALO = 16
MLSTM_AUG = 16
FFN_BATCH_TILE = 2
MLSTM_UNROLL = 4
ATTN_Q_TILE = 1024
LANE = 128
A_QK = 2 * LANE
ZF_WIDTH = A_QRANK + A_KVRANK + LANE
ZF_KR = A_QRANK + A_KVRANK
VMEM_LIMIT = 48 * 1024 * 1024


def _dot(a, b):
    return jnp.dot(a, b, preferred_element_type=F32)


def _rms(x, g):
    return x * lax.rsqrt(jnp.mean(x * x, axis=-1, keepdims=True) + EPS) * g


def _norm_mod(x, g, shift, scale):
    return _rms(x, g) * (1 + scale) + shift


def _params(n_axes, vmem=VMEM_LIMIT):
    return pltpu.CompilerParams(dimension_semantics=("arbitrary",) * n_axes, vmem_limit_bytes=vmem)


def _resident(block, index_map):
    return pl.BlockSpec(block, index_map, pipeline_mode=pl.Buffered(1))


def _mod_kernel(c_ref, w_ref, b_ref, o_ref):
    cv = c_ref[...]
    s = (cv * jax.nn.sigmoid(cv)).astype(BF16)
    o_ref[...] = _dot(s, w_ref[...].astype(BF16)) + b_ref[...]


def _modulations(cvec, w_ada, b_ada):
    L, D, N = w_ada.shape
    tn = 2304
    assert N % tn == 0 and tn % LANE == 0
    return pl.pallas_call(
        _mod_kernel,
        grid=(L, N // tn),
        in_specs=[
            pl.BlockSpec((8, D), lambda l, j: (0, 0)),
            pl.BlockSpec((None, D, tn), lambda l, j: (l, 0, j)),
            pl.BlockSpec((None, 1, tn), lambda l, j: (l, 0, j)),
        ],
        out_specs=pl.BlockSpec((None, 8, tn), lambda l, j: (l, 0, j)),
        out_shape=jax.ShapeDtypeStruct((L, 8, N), F32),
        compiler_params=_params(2),
        name="adaln_mod",
    )(cvec, w_ada, b_ada.reshape(L, 1, N))


def _ffn_kernel(*refs, d_ff, n_chunks, n_x_tiles, split_input, final_norm):
    refs = list(refs)
    o_ref = refs.pop()
    gf_ref = refs.pop() if final_norm else None
    if split_input:
        x = jnp.where(pl.program_id(1) < n_x_tiles, refs[0][...], refs[1][...])
        refs = refs[2:]
    else:
        x = refs[0][...]
        refs = refs[1:]
    mod_ref, g_ref, wup_ref, wdn_ref = refs
    mod = mod_ref[...]
    nb, rows, dm = x.shape
    h = _norm_mod(x, g_ref[...], mod[:, 0:1], mod[:, 1:2]).astype(BF16).reshape(nb * rows, dm)
    tf = d_ff // n_chunks
    acc = None
    for ci in range(n_chunks):
        a = _dot(h, wup_ref[:, ci * tf:(ci + 1) * tf])
        b = _dot(h, wup_ref[:, d_ff + ci * tf:d_ff + (ci + 1) * tf])
        act = (a * jax.nn.sigmoid(a) * b).astype(BF16)
        part = _dot(act, wdn_ref[ci * tf:(ci + 1) * tf, :])
        acc = part if acc is None else acc + part
    y = x + (0.5 * mod[:, 2:3]) * acc.reshape(nb, rows, dm)
    o_ref[...] = _rms(y, gf_ref[...]) if final_norm else y


def _ffn(src, mods, g, w_up, w_dn, l, sub, n_x_tiles, final_g=None):
    split_input = isinstance(src, tuple)
    nbt = FFN_BATCH_TILE
    if split_input:
        xs, cs = src
        B, S, D = xs.shape
        T = S + cs.shape[1]
        row_specs = [
            pl.BlockSpec((nbt, TOKEN_TILE, D), lambda b, t: (b, jnp.minimum(t, n_x_tiles - 1), 0)),
            pl.BlockSpec((nbt, TOKEN_TILE, D), lambda b, t: (b, 0, 0)),
        ]
        rows = [xs, cs]
    else:
        B, T, D = src.shape
        row_specs = [pl.BlockSpec((nbt, TOKEN_TILE, D), lambda b, t: (b, t, 0))]
        rows = [src]
    d_ff = w_dn.shape[1]
    final_norm = final_g is not None
    nt = n_x_tiles if final_norm else T // TOKEN_TILE
    who = lambda b, t: jnp.where(t >= n_x_tiles, B // nbt, b)
    extra_specs = [pl.BlockSpec((1, D), lambda b, t: (0, 0))] if final_norm else []
    extra = [final_g.reshape(1, D)] if final_norm else []
    return pl.pallas_call(
        functools.partial(_ffn_kernel, d_ff=d_ff, n_chunks=1, n_x_tiles=n_x_tiles, split_input=split_input,
                          final_norm=final_norm),
        grid=(B // nbt, nt),
        in_specs=row_specs + [
            pl.BlockSpec((None, nbt, None, 3, D), lambda b, t: (l, who(b, t), sub, 0, 0)),
            pl.BlockSpec((None, 1, D), lambda b, t: (l, 0, 0)),
            _resident((None, D, 2 * d_ff), lambda b, t: (l, 0, 0)),
            _resident((None, d_ff, D), lambda b, t: (l, 0, 0)),
        ] + extra_specs,
        out_specs=pl.BlockSpec((nbt, TOKEN_TILE, D), lambda b, t: (b, t, 0)),
        out_shape=jax.ShapeDtypeStruct((B, nt * TOKEN_TILE, D), F32),
        compiler_params=pltpu.CompilerParams(
            dimension_semantics=("arbitrary",) * 2, vmem_limit_bytes=VMEM_LIMIT,
            allow_input_fusion=[False] * (len(rows) + 2) + [True, True] + [False] * len(extra)),
        name="ffn_half_step",
    )(*rows, mods, g, w_up, w_dn, *extra)


def _proj_kernel(r_ref, rp_ref, rn_ref, mod_ref, g_ref, wqk_ref, wv_ref, wb_ref, wf_ref, wg_ref, bg_ref, wc_ref,
                 qt_ref, k_ref, vt_ref, zb_ref, zf_ref, gt_ref, hfull_ref, *, n_x_tiles, n_tiles):
    mod = mod_ref[...]
    g = g_ref[...]
    nm = lambda x: _norm_mod(x, g, mod[:, 0:1], mod[:, 1:2])
    nb, tile, dm = r_ref.shape
    t = pl.program_id(1)
    has_prev = jnp.logical_and(t != 0, t != n_x_tiles)
    has_next = jnp.logical_and(t != n_x_tiles - 1, t != n_tiles - 1)
    hm3 = nm(r_ref[...]).astype(BF16)
    hp3 = jnp.where(has_prev, nm(rp_ref[...]), 0.0).astype(BF16)
    hn3 = jnp.where(has_next, nm(rn_ref[...]), 0.0).astype(BF16)
    span = tile + 2 * HALO
    for i in range(nb):
        hfull_ref[i * span:i * span + HALO, :] = hp3[i]
        hfull_ref[i * span + HALO:i * span + HALO + tile, :] = hm3[i]
        hfull_ref[i * span + HALO + tile:(i + 1) * span, :] = hn3[i]
    hfull = hfull_ref[...]
    hm = hm3.reshape(nb * tile, dm)
    rows = nb * span
    width = wqk_ref.shape[1]
    cw = 512
    for ci in range(width // cw):
        cs = slice(ci * cw, (ci + 1) * cw)
        z = _dot(hfull, wqk_ref[:, cs])
        w = wc_ref[:, cs]
        cv = w[0:1] * pltpu.roll(z, 1, 0) + w[1:2] * z + w[2:3] * pltpu.roll(z, rows - 1, 0)
        y = cv * jax.nn.sigmoid(cv)
        for i in range(nb):
            yi = y[i * span + HALO:i * span + HALO + tile]
            if ci * cw < width // 2:
                qt_ref[i, cs, :] = (yi * (M_HEAD_DIM ** -0.5)).astype(BF16).T
            else:
                k_ref[i, :, ci * cw - width // 2:(ci + 1) * cw - width // 2] = yi.astype(BF16)
    v = _dot(hm, wv_ref[...]).astype(BF16)
    zg = _dot(hm, wg_ref[...]) + bg_ref[...]
    for i in range(nb):
        vt_ref[i] = v[i * tile:(i + 1) * tile].T
        gt_ref[i] = zg[i * tile:(i + 1) * tile].T[0:gt_ref.shape[1], :]
    zb_ref[...] = _dot(hm, wb_ref[...]).astype(BF16).reshape(zb_ref.shape)
    zf_ref[...] = _dot(hm, wf_ref[...]).reshape(zf_ref.shape)


def _project(R, mods, g, wqk, wv, wb, wf, wg, bg, wconv, l, n_x_tiles):
    B, T, D = R.shape
    nt = T // TOKEN_TILE
    hb = TOKEN_TILE // HALO
    n_halo_blocks = T // HALO
    nbt = FFN_BATCH_TILE
    who = lambda b, t: jnp.where(t >= n_x_tiles, B // nbt, b)
    nqk, nv, nb, nf, ng = wqk.shape[2], wv.shape[2], wb.shape[2], wf.shape[2], wg.shape[2]
    nq = nqk // 2
    n_gates = 4 * M_HEADS
    return pl.pallas_call(
        functools.partial(_proj_kernel, n_x_tiles=n_x_tiles, n_tiles=nt),
        grid=(B // nbt, nt),
        in_specs=[
            pl.BlockSpec((nbt, TOKEN_TILE, D), lambda b, t: (b, t, 0)),
            pl.BlockSpec((nbt, HALO, D), lambda b, t: (b, jnp.maximum(t * hb - 1, 0), 0)),
            pl.BlockSpec((nbt, HALO, D), lambda b, t: (b, jnp.minimum((t + 1) * hb, n_halo_blocks - 1), 0)),
            pl.BlockSpec((None, nbt, None, 3, D), lambda b, t: (l, who(b, t), 1, 0, 0)),
            pl.BlockSpec((None, 1, D), lambda b, t: (l, 0, 0)),
            _resident((None, D, nqk), lambda b, t: (l, 0, 0)),
            _resident((None, D, nv), lambda b, t: (l, 0, 0)),
            _resident((None, D, nb), lambda b, t: (l, 0, 0)),
            _resident((None, D, nf), lambda b, t: (l, 0, 0)),
            _resident((None, D, ng), lambda b, t: (l, 0, 0)),
            pl.BlockSpec((None, 1, ng), lambda b, t: (l, 0, 0)),
            pl.BlockSpec((None, M_CONV, nqk), lambda b, t: (l, 0, 0)),
        ],
        out_specs=[
            pl.BlockSpec((nbt, nq, TOKEN_TILE), lambda b, t: (b, 0, t)),
            pl.BlockSpec((nbt, TOKEN_TILE, nq), lambda b, t: (b, t, 0)),
            pl.BlockSpec((nbt, nv, TOKEN_TILE), lambda b, t: (b, 0, t)),
            pl.BlockSpec((nbt, TOKEN_TILE, nb), lambda b, t: (b, t, 0)),
            pl.BlockSpec((nbt, TOKEN_TILE, nf), lambda b, t: (b, t, 0)),
            pl.BlockSpec((nbt, n_gates, TOKEN_TILE), lambda b, t: (b, 0, t)),
        ],
        out_shape=[
            jax.ShapeDtypeStruct((B, nq, T), BF16),
            jax.ShapeDtypeStruct((B, T, nq), BF16),
            jax.ShapeDtypeStruct((B, nv, T), BF16),
            jax.ShapeDtypeStruct((B, T, nb), BF16),
            jax.ShapeDtypeStruct((B, T, nf), F32),
            jax.ShapeDtypeStruct((B, n_gates, T), F32),
        ],
        scratch_shapes=[pltpu.VMEM((nbt * (TOKEN_TILE + 2 * HALO), D), BF16)],
        compiler_params=_params(2),
        name="in_projection",
    )(R, R, R, mods, g, wqk, wv, wb, wf, wg, bg, wconv)


def _mlstm_kernel(qt_ref, k_ref, vt_ref, gr_ref, o_ref, c_ref, pre_ref, *, n_x, seq):
    L = TOKEN_TILE
    row = lax.broadcasted_iota(jnp.int32, (L, L), 0)
    col = lax.broadcasted_iota(jnp.int32, (L, L), 1)
    lower = col <= row
    upper = col >= row
    diag = col == row
    ones_rows = jnp.ones((MLSTM_AUG, L), BF16)
    log2e = math.log2(math.e)

    def step(d, c0, m, accumulate):
        feeds = upper if d == 0 else lower
        sl = pl.ds(c0, L)
        qt, k = qt_ref[:, sl], k_ref[sl, :]
        vt = jnp.concatenate([vt_ref[:, sl], ones_rows], axis=0)
        b_row = pre_ref[2 * d, 0:1, sl]
        g_row = pre_ref[2 * d + 1, 0:1, sl]
        b_end = jnp.min(b_row, axis=1, keepdims=True)
        g_col = jnp.sum(jnp.where(diag, g_row, 0.0), axis=1, keepdims=True)
        g_masked = jnp.where(feeds, g_col, -jnp.inf)
        inter = b_row + m
        mj = jnp.maximum(inter, b_row + jnp.max(g_masked, axis=0, keepdims=True))
        st = _dot(k, qt) * jnp.exp2(g_masked + (b_row - mj))
        e_inter = jnp.exp2(inter - mj)
        c_old = c_ref[d]
        num = _dot(vt, st.astype(BF16)) + e_inter * _dot(c_old.astype(BF16), qt)
        den = num[M_HEAD_DIM:M_HEAD_DIM + 1]
        ht = num[:M_HEAD_DIM] * (1.0 / jnp.maximum(jnp.abs(den), jnp.exp2(-mj)))
        if accumulate:
            o_ref[:, sl] += ht
        else:
            o_ref[:, sl] = ht
        m_new = jnp.maximum(b_end + m, b_end + jnp.max(g_row, axis=1, keepdims=True))
        w_end = jnp.exp2(b_end + g_row - m_new)
        c_ref[d] = jnp.exp2(b_end + m - m_new) * c_old + _dot((vt.astype(F32) * w_end).astype(BF16), k)
        return m_new

    for c in range(pre_ref.shape[2] // L):
        cs = slice(c * L, (c + 1) * L)
        gr = gr_ref[:, cs]
        for d in range(2):
            lf8 = jnp.broadcast_to(jax.nn.log_sigmoid(gr[2 * d + 1:2 * d + 2]) * log2e, (8, L))
            lf_hi = lf8.astype(BF16)
            lf_lo = (lf8 - lf_hi.astype(F32)).astype(BF16)
            feeds_bf = jnp.where(upper if d == 0 else lower, 1.0, 0.0).astype(BF16)
            b8 = _dot(lf_hi, feeds_bf) + _dot(lf_lo, feeds_bf)
            pre_ref[2 * d, :, cs] = b8
            pre_ref[2 * d + 1, :, cs] = gr[2 * d:2 * d + 1] * log2e - b8

    c_ref[...] = jnp.zeros(c_ref.shape, F32)
    m0 = jnp.zeros((1, 1), F32)
    mf = step(0, seq, m0, False)
    mb = step(1, seq, m0, True)

    def both(accumulate):
        def body(i, carry):
            mf, mb = carry
            for u in range(MLSTM_UNROLL):
                ci = i * MLSTM_UNROLL + u
                mf = step(0, pl.multiple_of(ci * L, L), mf, accumulate)
                mb = step(1, pl.multiple_of((n_x - 1 - ci) * L, L), mb, accumulate)
            return mf, mb
        return body

    trips = n_x // MLSTM_UNROLL
    carry = lax.fori_loop(0, trips // 2, both(False), (mf, mb))
    lax.fori_loop(trips // 2, trips, both(True), carry)


def _mlstm(QT, K, VT, grow, seq):
    B, T, _ = K.shape
    dh = M_HEAD_DIM
    n_x = seq // TOKEN_TILE
    return pl.pallas_call(
        functools.partial(_mlstm_kernel, n_x=n_x, seq=seq),
        grid=(B, M_HEADS),
        in_specs=[
            pl.BlockSpec((None, dh, T), lambda b, h: (b, h, 0)),
            pl.BlockSpec((None, T, dh), lambda b, h: (b, 0, h)),
            pl.BlockSpec((None, dh, T), lambda b, h: (b, h, 0)),
            pl.BlockSpec((None, None, 4, T), lambda b, h: (b, h, 0, 0)),
        ],
        out_specs=pl.BlockSpec((None, dh, T), lambda b, h: (b, h, 0)),
        out_shape=jax.ShapeDtypeStruct((B, M_HEADS * dh, T), F32),
        scratch_shapes=[pltpu.VMEM((2, dh + MLSTM_AUG, dh), F32), pltpu.VMEM((4, 8, T), F32)],
        compiler_params=_params(2),
        name="mlstm_scan",
    )(QT, K, VT, grow)


def _mla_prep_kernel(zf_ref, cos_ref, sa_ref, sb_ref, gq_ref, gkv_ref, wuq_ref, wuk_ref, wuv_ref,
                     aq_ref, ak_ref, av_ref, *, scale):
    nb, rows, width = zf_ref.shape
    zf = zf_ref[...].reshape(nb * rows, width)
    cqn = _rms(zf[:, :A_QRANK], gq_ref[...]).astype(BF16)
    ckvn = _rms(zf[:, A_QRANK:ZF_KR], gkv_ref[...]).astype(BF16)
    cos, sa, sb = cos_ref[...], sa_ref[...], sb_ref[...]

    def rope(x):
        return x * cos + pltpu.roll(x, A_ROPE // 2, 1) * sa + pltpu.roll(x, LANE - A_ROPE // 2, 1) * sb

    qf = _dot(cqn, wuq_ref[...])
    kn = _dot(ckvn, wuk_ref[...])
    av = _dot(ckvn, wuv_ref[...]).astype(BF16)
    for i in range(nb):
        rs = slice(i * rows, (i + 1) * rows)
        for h in range(A_HEADS):
            o = h * A_QK
            aq_ref[i, :, o:o + A_NOPE] = (qf[rs, o:o + A_NOPE] * scale).astype(BF16)
            aq_ref[i, :, o + A_NOPE:o + A_QK] = (rope(qf[rs, o + A_NOPE:o + A_QK]) * scale).astype(BF16)
        kr = rope(zf[rs, ZF_KR:ZF_KR + LANE])
        for h in range(A_HEADS):
            kh = jnp.concatenate([kn[rs, h * A_NOPE:(h + 1) * A_NOPE], kr], axis=1)
            ak_ref[i, h * A_QK:(h + 1) * A_QK, :] = kh.T.astype(BF16)
        av_ref[i] = av[rs]


def _mla_prep(ZF, cos_t, sin_a, sin_b, gq, gkv, wuq, wuk, wuv, l):
    B, T, _ = ZF.shape
    nt = T // TOKEN_TILE
    nbt = FFN_BATCH_TILE
    scale = (A_NOPE + A_ROPE) ** -0.5 * math.log2(math.e)
    return pl.pallas_call(
        functools.partial(_mla_prep_kernel, scale=scale),
        grid=(B // nbt, nt),
        in_specs=[
            pl.BlockSpec((nbt, TOKEN_TILE, ZF_WIDTH), lambda b, t: (b, t, 0)),
            pl.BlockSpec((TOKEN_TILE, LANE), lambda b, t: (t, 0)),
            pl.BlockSpec((TOKEN_TILE, LANE), lambda b, t: (t, 0)),
            pl.BlockSpec((TOKEN_TILE, LANE), lambda b, t: (t, 0)),
            pl.BlockSpec((None, 1, A_QRANK), lambda b, t: (l, 0, 0)),
            pl.BlockSpec((None, 1, A_KVRANK), lambda b, t: (l, 0, 0)),
            _resident((None, A_QRANK, A_HEADS * A_QK), lambda b, t: (l, 0, 0)),
            _resident((None, A_KVRANK, A_HEADS * A_NOPE), lambda b, t: (l, 0, 0)),
            _resident((None, A_KVRANK, A_HEADS * A_VDIM), lambda b, t: (l, 0, 0)),
        ],
        out_specs=[
            pl.BlockSpec((nbt, TOKEN_TILE, A_HEADS * A_QK), lambda b, t: (b, t, 0)),
            pl.BlockSpec((nbt, A_HEADS * A_QK, TOKEN_TILE), lambda b, t: (b, 0, t)),
            pl.BlockSpec((nbt, TOKEN_TILE, A_HEADS * A_VDIM), lambda b, t: (b, t, 0)),
        ],
        out_shape=[
            jax.ShapeDtypeStruct((B, T, A_HEADS * A_QK), BF16),
            jax.ShapeDtypeStruct((B, A_HEADS * A_QK, T), BF16),
            jax.ShapeDtypeStruct((B, T, A_HEADS * A_VDIM), BF16),
        ],
        compiler_params=_params(2),
        name="mla_up_projection",
    )(ZF, cos_t, sin_a, sin_b, gq, gkv, wuq, wuk, wuv)


def _attn_kernel(q_ref, kt_ref, v_ref, o_ref, vaug_ref, *, n_split):
    nk = v_ref.shape[0]

    @pl.when(pl.program_id(2) == 0)
    def _():
        vaug_ref[:, :A_VDIM] = v_ref[...]
        lane = lax.broadcasted_iota(jnp.int32, (nk, LANE), 1)
        vaug_ref[:, A_VDIM:] = jnp.where(lane == 0, 1.0, 0.0).astype(BF16)

    rows = q_ref.shape[0] // n_split
    scores = [_dot(q_ref[r * rows:(r + 1) * rows, :], kt_ref[...]) for r in range(n_split)]
    for r, s in enumerate(scores):
        rs = slice(r * rows, (r + 1) * rows)
        p = jnp.exp2(s - jnp.max(s, axis=1, keepdims=True)).astype(BF16)
        acc = _dot(p, vaug_ref[...])
        o_ref[rs, :] = (acc[:, :A_VDIM] * (1.0 / acc[:, A_VDIM:A_VDIM + 1])).astype(o_ref.dtype)


def _ctx_attn_kernel(q_ref, kt_ref, v_ref, o_ref):
    for h in range(A_HEADS):
        s = _dot(q_ref[:, h * A_QK:(h + 1) * A_QK], kt_ref[h * A_QK:(h + 1) * A_QK, :])
        p = jnp.exp2(s - jnp.max(s, axis=1, keepdims=True))
        acc = _dot(p.astype(BF16), v_ref[:, h * A_VDIM:(h + 1) * A_VDIM])
        o_ref[:, h * A_VDIM:(h + 1) * A_VDIM] = (acc * (1.0 / jnp.sum(p, axis=1, keepdims=True))).astype(o_ref.dtype)


def _attention(AQ, AK, AV, seq, with_ctx):
    B, T, _ = AQ.shape
    ctx_len = T - seq
    tq = ATTN_Q_TILE
    width = A_HEADS * A_VDIM
    a_lat = pl.pallas_call(
        functools.partial(_attn_kernel, n_split=2),
        grid=(B, A_HEADS, seq // tq),
        in_specs=[
            pl.BlockSpec((None, tq, A_QK), lambda b, h, t: (b, t, h)),
            pl.BlockSpec((None, A_QK, T), lambda b, h, t: (b, h, 0)),
            pl.BlockSpec((None, T, A_VDIM), lambda b, h, t: (b, 0, h)),
        ],
        out_specs=pl.BlockSpec((None, tq, A_VDIM), lambda b, h, t: (b, t, h)),
        out_shape=jax.ShapeDtypeStruct((B, seq, width), BF16),
        scratch_shapes=[pltpu.VMEM((T, A_VDIM + LANE), BF16)],
        compiler_params=_params(3),
        name="mla_attention_latent",
    )(AQ, AK, AV)
    if not with_ctx:
        return a_lat, None
    cb = seq // ctx_len
    a_ctx = pl.pallas_call(
        _ctx_attn_kernel,
        grid=(B,),
        in_specs=[
            pl.BlockSpec((None, ctx_len, A_HEADS * A_QK), lambda b: (b, cb, 0)),
            pl.BlockSpec((None, A_HEADS * A_QK, ctx_len), lambda b: (b, 0, cb)),
            pl.BlockSpec((None, ctx_len, width), lambda b: (b, cb, 0)),
        ],
        out_specs=pl.BlockSpec((None, ctx_len, width), lambda b: (b, 0, 0)),
        out_shape=jax.ShapeDtypeStruct((B, ctx_len, width), BF16),
        compiler_params=_params(1),
        name="mla_attention_context",
    )(AQ, AK, AV)
    return a_lat, a_ctx


def _merge_kernel(ht_ref, mo_ref, bg_ref, al_ref, ac_ref, r_ref, mod_ref, gmh_ref, wbm_ref, wba_ref, wout_ref, o_ref,
                  *, n_x_tiles):
    nb, rows, dm = r_ref.shape
    a = jnp.where(pl.program_id(1) < n_x_tiles, al_ref[...], ac_ref[...]).reshape(nb * rows, al_ref.shape[2])
    dh = M_HEAD_DIM

    def head_norm_t(b, i):
        x = ht_ref[b, i * dh:(i + 1) * dh, :]
        return x * lax.rsqrt(jnp.mean(x * x, axis=0, keepdims=True) + EPS)

    hn = jnp.concatenate(
        [jnp.concatenate([head_norm_t(b, i) for i in range(M_HEADS)], axis=0).T for b in range(nb)], axis=0) * gmh_ref[...]
    width = hn.shape[1]
    hm = (jax.nn.sigmoid(mo_ref[...].reshape(nb * rows, width).astype(F32)) * hn).astype(BF16)
    bg = bg_ref[...].reshape(nb * rows, 2 * width).astype(F32)
    u = jax.nn.sigmoid(bg[:, :width]) * _dot(hm, wbm_ref[...]) + jax.nn.sigmoid(bg[:, width:]) * _dot(a, wba_ref[...])
    y = _dot(u.astype(BF16), wout_ref[...])
    o_ref[...] = r_ref[...] + mod_ref[...][:, 2:3] * y.reshape(nb, rows, dm)


def _merge(HT, ZB, A_lat, A_ctx, R, mods, gmh, wbm, wba, wout, l, n_x_tiles, latent_only):
    B, T, D = R.shape
    nt = n_x_tiles if latent_only else T // TOKEN_TILE
    W = HT.shape[1]
    AW = A_lat.shape[2]
    assert W == D
    nbt = FFN_BATCH_TILE
    who = lambda b, t: jnp.where(t >= n_x_tiles, B // nbt, b)
    return pl.pallas_call(
        functools.partial(_merge_kernel, n_x_tiles=n_x_tiles),
        grid=(B // nbt, nt),
        in_specs=[
            pl.BlockSpec((nbt, W, TOKEN_TILE), lambda b, t: (b, 0, t)),
            pl.BlockSpec((nbt, TOKEN_TILE, W), lambda b, t: (b, t, 2)),
            pl.BlockSpec((nbt, TOKEN_TILE, 2 * D), lambda b, t: (b, t, 0)),
            pl.BlockSpec((nbt, TOKEN_TILE, AW), lambda b, t: (b, jnp.minimum(t, n_x_tiles - 1), 0)),
            pl.BlockSpec((nbt, TOKEN_TILE, AW), lambda b, t: (b, 0, 0)),
            pl.BlockSpec((nbt, TOKEN_TILE, D), lambda b, t: (b, t, 0)),
            pl.BlockSpec((None, nbt, None, 3, D), lambda b, t: (l, who(b, t), 1, 0, 0)),
            pl.BlockSpec((None, 1, W), lambda b, t: (l, 0, 0)),
            _resident((None, W, D), lambda b, t: (l, 0, 0)),
            _resident((None, AW, D), lambda b, t: (l, 0, 0)),
            _resident((None, D, D), lambda b, t: (l, 0, 0)),
        ],
        out_specs=pl.BlockSpec((nbt, TOKEN_TILE, D), lambda b, t: (b, t, 0)),
        out_shape=jax.ShapeDtypeStruct((B, nt * TOKEN_TILE, D), F32),
        compiler_params=_params(2),
        name="mixer_merge",
    )(HT, ZB, ZB, A_lat, A_ctx, R, mods, gmh, wbm, wba, wout)


def _rope_tables(seq, ctx_len):
    half = A_ROPE // 2
    axis_rot = A_ROPE // 2
    pos = jnp.arange(seq)
    rowp = (pos // GRID_W).astype(F32)
    colp = (pos % GRID_W).astype(F32)
    inv = ROPE_THETA ** (-jnp.arange(0, axis_rot, 2, dtype=F32) / axis_rot)
    ang = jnp.concatenate([rowp[:, None] * inv, colp[:, None] * inv], axis=-1)
    cos, sin = jnp.cos(ang), jnp.sin(ang)
    zeros = jnp.zeros((seq, half), F32)
    pad = jnp.zeros((seq, LANE - A_ROPE), F32)
    cos_x = jnp.concatenate([cos, cos, pad], axis=1)
    sa_x = jnp.concatenate([zeros, sin, pad], axis=1)
    sb_x = jnp.concatenate([-sin, zeros, pad], axis=1)
    cos_c = jnp.concatenate([jnp.ones((ctx_len, A_ROPE), F32), jnp.zeros((ctx_len, LANE - A_ROPE), F32)], axis=1)
    zc = jnp.zeros((ctx_len, LANE), F32)
    return (jnp.concatenate([cos_x, cos_c], axis=0), jnp.concatenate([sa_x, zc], axis=0),
            jnp.concatenate([sb_x, zc], axis=0))


def kernel(x, c, ctx, c_ctx, w_ada, b_ada, g_n1, g_n2, g_n3, w_ff1_up, w_ff1_dn, w_ff2_up, w_ff2_dn, w_in, b_gate, w_conv, g_mh, g_qa, g_kva, w_uq, w_ukv, w_bm, w_ba, w_out, g_final):
    B, S, D = x.shape
    C = ctx.shape[1]
    L = w_ada.shape[0]
    T = S + C
    MW = M_HEADS * M_HEAD_DIM
    assert C == TOKEN_TILE and S % (2 * MLSTM_UNROLL * TOKEN_TILE) == 0 and S % ATTN_Q_TILE == 0 and S % GRID_W == 0
    assert B % FFN_BATCH_TILE == 0 and B + FFN_BATCH_TILE <= 8 and D == MW
    n_x_tiles = S // TOKEN_TILE

    perm = jnp.concatenate([jnp.arange(0, A_ROPE, 2), jnp.arange(1, A_ROPE, 2)])
    o_gate = 4 * MW
    o_cq = o_gate + 4 * M_HEADS
    o_ckv = o_cq + A_QRANK
    o_kr = o_ckv + A_KVRANK
    o_br = o_kr + A_ROPE
    n_gates = 4 * M_HEADS
    gate_perm = jnp.array([dk * M_HEADS + h for h in range(M_HEADS) for dk in range(4)])
    wqk = w_in[:, :, :2 * MW].astype(BF16)
    wv = w_in[:, :, 2 * MW:3 * MW].astype(BF16)
    wb = jnp.concatenate([w_in[:, :, o_br:], w_in[:, :, 3 * MW:4 * MW]], axis=2).astype(BF16)
    wf = jnp.concatenate([
        w_in[:, :, o_cq:o_kr], w_in[:, :, o_kr:o_br][:, :, perm], jnp.zeros((L, D, LANE - A_ROPE), F32)],
        axis=2).astype(BF16)
    wg = jnp.concatenate([w_in[:, :, o_gate:o_cq][:, :, gate_perm], jnp.zeros((L, D, LANE - n_gates), F32)],
                         axis=2).astype(BF16)
    bg = jnp.concatenate([b_gate[:, gate_perm], jnp.zeros((L, LANE - n_gates), F32)], axis=1).reshape(L, 1, LANE)
    dq = A_NOPE + A_ROPE
    wuq4 = w_uq.reshape(L, A_QRANK, A_HEADS, dq)
    wuq = jnp.concatenate([wuq4[..., :A_NOPE], wuq4[..., A_NOPE:][..., perm],
                           jnp.zeros((L, A_QRANK, A_HEADS, A_QK - dq), F32)], axis=-1)
    wuq = wuq.reshape(L, A_QRANK, A_HEADS * A_QK).astype(BF16)
    wukv4 = w_ukv.reshape(L, A_KVRANK, A_HEADS, A_NOPE + A_VDIM)
    wuk = wukv4[..., :A_NOPE].reshape(L, A_KVRANK, A_HEADS * A_NOPE).astype(BF16)
    wuv = wukv4[..., A_NOPE:].reshape(L, A_KVRANK, A_HEADS * A_VDIM).astype(BF16)
    wup1, wdn1 = w_ff1_up.astype(BF16), w_ff1_dn.astype(BF16)
    wup2, wdn2 = w_ff2_up.astype(BF16), w_ff2_dn.astype(BF16)
    wbm, wba, wout = w_bm.astype(BF16), w_ba.astype(BF16), w_out.astype(BF16)
    row3 = lambda a: a.reshape(L, 1, a.shape[1])
    gn1, gn2, gn3, gmh, gqa, gkva = map(row3, (g_n1, g_n2, g_n3, g_mh, g_qa, g_kva))
    cos_t, sin_a, sin_b = _rope_tables(S, C)

    cvec = jnp.concatenate([c] + [c_ctx[None, :]] * FFN_BATCH_TILE + [jnp.zeros((8 - B - FFN_BATCH_TILE, D), F32)], axis=0)
    mods = _modulations(cvec, w_ada, b_ada).reshape(L, 8, N_MOD // 3, 3, D)

    R = (x, ctx)
    for l in range(L):
        R = _ffn(R, mods, gn1, wup1, wdn1, l, 0, n_x_tiles)
        QT, K, VT, ZB, ZF, GT = _project(R, mods, gn2, wqk, wv, wb, wf, wg, bg, w_conv, l, n_x_tiles)
        HT = _mlstm(QT, K, VT, GT.reshape(B, M_HEADS, 4, T), S)
        AQ, AK, AV = _mla_prep(ZF, cos_t, sin_a, sin_b, gqa, gkva, wuq, wuk, wuv, l)
        last = l == L - 1
        A_lat, A_ctx = _attention(AQ, AK, AV, S, with_ctx=not last)
        R = _merge(HT, ZB, A_lat, A_lat if last else A_ctx, R, mods, gmh, wbm, wba, wout, l, n_x_tiles, latent_only=last)
        R = _ffn(R, mods, gn3, wup2, wdn2, l, 2, n_x_tiles, final_g=g_final if last else None)
    return R
```
